```python
import math
import jax, jax.numpy as jnp
from jax import lax
import numpy as np

D_MODEL = 1024
BATCH = 16
SEQ = 4096
DEPTH = 1

D_MIX = D_MODEL
D_SSM = D_MIX // 2
SSM_GROUP = 16
N_SSM_GROUPS = D_SSM // SSM_GROUP
SSM_STATE = 64
D_DN = D_MIX - D_SSM
DN_HEAD_DIM = 128
N_DN_HEADS = D_DN // DN_HEAD_DIM
CONV_K = 5
CHUNK = 64
NORM_EPS = 1e-6
DT_MIN = 1e-3
DT_MAX = 1e-1
IN_COLS = 2 * D_SSM + 4 * D_DN + 4 * N_DN_HEADS

kernel_name = "hybrid_s5_gdn_adaln_block"


def rms_norm(x, w):
    x32 = x.astype(jnp.float32)
    y = x32 * lax.rsqrt(jnp.mean(x32 * x32, axis=-1, keepdims=True) + NORM_EPS)
    return (y * w.astype(jnp.float32)).astype(x.dtype)


def l2_normalize(t):
    return t * lax.rsqrt(jnp.sum(t * t, axis=-1, keepdims=True) + NORM_EPS)


def centred_depthwise_conv(t, w):
    ch = t.shape[-1]
    return lax.conv_general_dilated(
        t, w[:, None, :].astype(t.dtype), window_strides=(1,),
        padding=[(CONV_K // 2, CONV_K // 2)],
        dimension_numbers=("NWC", "WIO", "NWC"), feature_group_count=ch)


def s5_discretize(lam_re, lam_im, log_step, b_re, b_im):
    f32 = jnp.float32
    dt = jnp.exp(log_step.astype(f32))[:, None]
    lr, li = lam_re.astype(f32), lam_im.astype(f32)
    mag = jnp.exp(lr * dt)
    ab_re, ab_im = mag * jnp.cos(li * dt), mag * jnp.sin(li * dt)
    nr, ni = ab_re - 1.0, ab_im
    den = lr * lr + li * li
    f_re = (nr * lr + ni * li) / den
    f_im = (ni * lr - nr * li) / den
    br, bi = b_re.astype(f32), b_im.astype(f32)
    bb_re = f_re[..., None] * br - f_im[..., None] * bi
    bb_im = f_re[..., None] * bi + f_im[..., None] * br
    return ab_re, ab_im, bb_re, bb_im


def s5_direction(u, lam_re, lam_im, log_step, b_re, b_im, c_re, c_im, reverse):
    ab_re, ab_im, bb_re, bb_im = s5_discretize(lam_re, lam_im, log_step, b_re, b_im)
    bu_re = jnp.einsum("blgp,gnp->blgn", u, bb_re)
    bu_im = jnp.einsum("blgp,gnp->blgn", u, bb_im)
    seq = u.shape[1]
    a_re = jnp.broadcast_to(ab_re, (1, seq) + ab_re.shape)
    a_im = jnp.broadcast_to(ab_im, (1, seq) + ab_im.shape)

    def combine(e1, e2):
        a1r, a1i, b1r, b1i = e1
        a2r, a2i, b2r, b2i = e2
        return (a1r * a2r - a1i * a2i,
                a1r * a2i + a1i * a2r,
                a2r * b1r - a2i * b1i + b2r,
                a2r * b1i + a2i * b1r + b2i)

    _, _, xr, xi = lax.associative_scan(combine, (a_re, a_im, bu_re, bu_im),
                                        reverse=reverse, axis=1)
    cr, ci = c_re.astype(jnp.float32), c_im.astype(jnp.float32)
    return jnp.einsum("blgn,gpn->blgp", xr, cr) - jnp.einsum("blgn,gpn->blgp", xi, ci)


def gated_delta_chunked(q, k, v, g, beta):
    bsz, nh, seq, dk = q.shape
    dv = v.shape[-1]
    n = seq // CHUNK
    rs = lambda t: t.reshape((bsz, nh, n, CHUNK) + t.shape[3:])
    q, k, v, g, beta = rs(q), rs(k), rs(v), rs(g), rs(beta)
    g = jnp.cumsum(g, axis=-1)
    idx = jnp.arange(CHUNK)
    lower = idx[:, None] >= idx[None, :]
    strict = idx[:, None] > idx[None, :]
    diff = g[..., :, None] - g[..., None, :]
    decay = jnp.where(lower, jnp.exp(jnp.where(lower, diff, 0.0)), 0.0)
    k_beta = k * beta[..., None]
    lmat = jnp.where(strict, jnp.einsum("bhnid,bhnjd->bhnij", k_beta, k) * decay, 0.0)
    tri = lmat + jnp.eye(CHUNK, dtype=lmat.dtype)
    rhs = jnp.concatenate([v * beta[..., None], k_beta * jnp.exp(g)[..., None]], axis=-1)
    sol = lax.linalg.triangular_solve(tri, rhs, left_side=True, lower=True,
                                      unit_diagonal=True)
    u_c, w_c = sol[..., :dv], sol[..., dv:]
    attn = jnp.where(lower, jnp.einsum("bhnid,bhnjd->bhnij", q, k) * decay, 0.0)
    g_last = g[..., -1]
    k_tail = k * jnp.exp(g_last[..., None] - g)[..., None]
    q_dec = q * jnp.exp(g)[..., None]

    def step(state, inp):
        qd, w, u, a, kt, gl = inp
        v_new = u - jnp.einsum("bhcd,bhde->bhce", w, state)
        o = jnp.einsum("bhcd,bhde->bhce", qd, state) + jnp.einsum("bhij,bhje->bhie", a, v_new)
        state = state * jnp.exp(gl)[..., None, None] + jnp.einsum("bhcd,bhce->bhde", kt, v_new)
        return state, o

    mv = lambda t: jnp.moveaxis(t, 2, 0)
    xs = (mv(q_dec), mv(w_c), mv(u_c), mv(attn), mv(k_tail), mv(g_last))
    s0 = jnp.zeros((bsz, nh, dk, dv), jnp.float32)
    _, o = lax.scan(step, s0, xs)
    return jnp.moveaxis(o, 0, 2).reshape(bsz, nh, seq, dv)


def hybrid_layer(x, c, ada_w, ada_b, norm_w, w_in, conv_w,
                 dn_a_log_f, dn_dt_bias_f, dn_a_log_b, dn_dt_bias_b, dn_norm_w,
                 lam_re_f, lam_im_f, log_step_f, b_re_f, b_im_f, c_re_f, c_im_f,
                 lam_re_b, lam_im_b, log_step_b, b_re_b, b_im_b, c_re_b, c_im_b,
                 s5_d, glu_w, glu_b, w_out):
    f32 = jnp.float32
    dt = x.dtype
    bsz, seq, _ = x.shape
    mod = jax.nn.silu(c) @ ada_w + ada_b
    shift, scale, gate = jnp.split(mod, 3, axis=-1)
    h = rms_norm(x, norm_w) * (1.0 + scale[:, None, :]) + shift[:, None, :]
    proj = h @ w_in
    o1 = D_SSM
    o2 = o1 + D_SSM
    o3 = o2 + 3 * D_DN
    o4 = o3 + D_DN
    o5 = o4 + 2 * N_DN_HEADS
    u_a, z_a, qkv, z_b, beta_fb, alpha_fb = jnp.split(proj, [o1, o2, o3, o4, o5], axis=-1)

    u = u_a.astype(f32).reshape(bsz, seq, N_SSM_GROUPS, SSM_GROUP)
    y = (s5_direction(u, lam_re_f, lam_im_f, log_step_f, b_re_f, b_im_f, c_re_f, c_im_f, False)
         + s5_direction(u, lam_re_b, lam_im_b, log_step_b, b_re_b, b_im_b, c_re_b, c_im_b, True)
         + s5_d.astype(f32).reshape(N_SSM_GROUPS, SSM_GROUP) * u)
    y = jax.nn.gelu(y.reshape(bsz, seq, D_SSM)).astype(dt)
    glu = y @ glu_w + glu_b
    y_a = glu[..., :D_SSM] * jax.nn.sigmoid(glu[..., D_SSM:])
    y_a = y_a * jax.nn.silu(z_a)

    qkv = jax.nn.silu(centred_depthwise_conv(qkv, conv_w))
    q, k, v = jnp.split(qkv, 3, axis=-1)
    to_heads = lambda t: t.reshape(bsz, seq, N_DN_HEADS, DN_HEAD_DIM).transpose(0, 2, 1, 3).astype(f32)
    q = l2_normalize(to_heads(q)) * (DN_HEAD_DIM ** -0.5)
    k = l2_normalize(to_heads(k))
    v = to_heads(v)
    beta = jax.nn.sigmoid(beta_fb.astype(f32)).transpose(0, 2, 1)
    alpha = alpha_fb.astype(f32).transpose(0, 2, 1)
    g_f = -jnp.exp(dn_a_log_f.astype(f32))[None, :, None] * jax.nn.softplus(
        alpha[:, :N_DN_HEADS] + dn_dt_bias_f.astype(f32)[None, :, None])
    g_b = -jnp.exp(dn_a_log_b.astype(f32))[None, :, None] * jax.nn.softplus(
        alpha[:, N_DN_HEADS:] + dn_dt_bias_b.astype(f32)[None, :, None])
    o_f = gated_delta_chunked(q, k, v, g_f, beta[:, :N_DN_HEADS])
    flip = lambda t: jnp.flip(t, axis=2)
    o_b = flip(gated_delta_chunked(flip(q), flip(k), flip(v), flip(g_b),
                                   flip(beta[:, N_DN_HEADS:])))
    o = (o_f + o_b).transpose(0, 2, 1, 3)
    o = o * lax.rsqrt(jnp.mean(o * o, axis=-1, keepdims=True) + NORM_EPS) * dn_norm_w.astype(f32)
    o = o * jax.nn.silu(z_b.astype(f32).reshape(bsz, seq, N_DN_HEADS, DN_HEAD_DIM))
    y_b = o.reshape(bsz, seq, D_DN).astype(dt)

    mix = jnp.concatenate([y_a, y_b], axis=-1) @ w_out
    return x + gate[:, None, :] * mix


def setup_inputs(seed: int = 0) -> dict:
    key = jax.random.key(seed)
    ks = iter(jax.random.split(key, 40))
    f32 = jnp.float32
    nrm = lambda shape, s: jax.random.normal(next(ks), shape, f32) * s
    G, N, P, H = N_SSM_GROUPS, SSM_STATE, SSM_GROUP, N_DN_HEADS

    def s5_params():
        lam_re = -0.5 + nrm((DEPTH, G, N), 0.01)
        lam_im = jnp.pi * jnp.arange(N, dtype=f32)[None, None, :] + nrm((DEPTH, G, N), 0.01)
        log_step = jax.random.uniform(next(ks), (DEPTH, G), f32, math.log(DT_MIN), math.log(DT_MAX))
        b_re = nrm((DEPTH, G, N, P), (2.0 * P) ** -0.5)
        b_im = nrm((DEPTH, G, N, P), (2.0 * P) ** -0.5)
        c_re = nrm((DEPTH, G, P, N), (2.0 * N) ** -0.5)
        c_im = nrm((DEPTH, G, P, N), (2.0 * N) ** -0.5)
        return lam_re, lam_im, log_step, b_re, b_im, c_re, c_im

    def dn_decay_params():
        a_log = jnp.log(jax.random.uniform(next(ks), (DEPTH, H), f32, 1.0, 16.0))
        dts = jnp.exp(jax.random.uniform(next(ks), (DEPTH, H), f32, math.log(DT_MIN), math.log(DT_MAX)))
        dt_bias = dts + jnp.log(-jnp.expm1(-dts))
        return a_log, dt_bias

    x = jax.random.normal(next(ks), (BATCH, SEQ, D_MODEL), f32)
    c = jax.random.normal(next(ks), (BATCH, D_MODEL), f32)
    ada_w = nrm((DEPTH, D_MODEL, 3 * D_MODEL), 0.5 * D_MODEL ** -0.5)
    ada_b = nrm((DEPTH, 3 * D_MODEL), 0.02)
    norm_w = 1.0 + nrm((DEPTH, D_MODEL), 0.02)
    w_in = nrm((DEPTH, D_MODEL, IN_COLS), D_MODEL ** -0.5)
    conv_w = nrm((DEPTH, CONV_K, 3 * D_DN), CONV_K ** -0.5)
    dn_a_log_f, dn_dt_bias_f = dn_decay_params()
    dn_a_log_b, dn_dt_bias_b = dn_decay_params()
    dn_norm_w = 1.0 + nrm((DEPTH, DN_HEAD_DIM), 0.02)
    lam_re_f, lam_im_f, log_step_f, b_re_f, b_im_f, c_re_f, c_im_f = s5_params()
    lam_re_b, lam_im_b, log_step_b, b_re_b, b_im_b, c_re_b, c_im_b = s5_params()
    s5_d = nrm((DEPTH, D_SSM), 1.0)
    glu_w = nrm((DEPTH, D_SSM, 2 * D_SSM), D_SSM ** -0.5)
    glu_b = nrm((DEPTH, 2 * D_SSM), 0.01)
    w_out = nrm((DEPTH, D_MIX, D_MODEL), D_MIX ** -0.5)
    final_norm_w = 1.0 + nrm((D_MODEL,), 0.02)
    return {"x": x, "c": c, "ada_w": ada_w, "ada_b": ada_b, "norm_w": norm_w,
            "w_in": w_in, "conv_w": conv_w,
            "dn_a_log_f": dn_a_log_f, "dn_dt_bias_f": dn_dt_bias_f,
            "dn_a_log_b": dn_a_log_b, "dn_dt_bias_b": dn_dt_bias_b, "dn_norm_w": dn_norm_w,
            "lam_re_f": lam_re_f, "lam_im_f": lam_im_f, "log_step_f": log_step_f,
            "b_re_f": b_re_f, "b_im_f": b_im_f, "c_re_f": c_re_f, "c_im_f": c_im_f,
            "lam_re_b": lam_re_b, "lam_im_b": lam_im_b, "log_step_b": log_step_b,
            "b_re_b": b_re_b, "b_im_b": b_im_b, "c_re_b": c_re_b, "c_im_b": c_im_b,
            "s5_d": s5_d, "glu_w": glu_w, "glu_b": glu_b, "w_out": w_out,
            "final_norm_w": final_norm_w}


def reference(x, c, ada_w, ada_b, norm_w, w_in, conv_w,
              dn_a_log_f, dn_dt_bias_f, dn_a_log_b, dn_dt_bias_b, dn_norm_w,
              lam_re_f, lam_im_f, log_step_f, b_re_f, b_im_f, c_re_f, c_im_f,
              lam_re_b, lam_im_b, log_step_b, b_re_b, b_im_b, c_re_b, c_im_b,
              s5_d, glu_w, glu_b, w_out, final_norm_w):
    for layer in range(DEPTH):
        x = hybrid_layer(
            x, c, ada_w[layer], ada_b[layer], norm_w[layer], w_in[layer], conv_w[layer],
            dn_a_log_f[layer], dn_dt_bias_f[layer], dn_a_log_b[layer], dn_dt_bias_b[layer],
            dn_norm_w[layer],
            lam_re_f[layer], lam_im_f[layer], log_step_f[layer], b_re_f[layer], b_im_f[layer],
            c_re_f[layer], c_im_f[layer],
            lam_re_b[layer], lam_im_b[layer], log_step_b[layer], b_re_b[layer], b_im_b[layer],
            c_re_b[layer], c_im_b[layer],
            s5_d[layer], glu_w[layer], glu_b[layer], w_out[layer])
    return rms_norm(x, final_norm_w)
```

```python
import functools
import math

import jax
import jax.numpy as jnp
from jax import lax
from jax.experimental import pallas as pl
from jax.experimental.pallas import tpu as pltpu

F32 = jnp.float32
BF16 = jnp.bfloat16
HIGHEST = lax.Precision.HIGHEST

NORM_EPS = 1e-6
SSM_GROUP = 16
SSM_STATE = 64
DN_HEAD_DIM = 128
CONV_K = 5
S5_T = 16
DN_CHUNK = 64
DN_SUB = 16
VMEM_LIMIT = 56 * 1024 * 1024


def _cparams(sem):
    return pltpu.CompilerParams(dimension_semantics=sem, vmem_limit_bytes=VMEM_LIMIT)


def _bdot(a, b):
    return jnp.dot(a.astype(BF16), b.astype(BF16), preferred_element_type=F32)


def _bdot_nt(a, b):
    return lax.dot_general(a.astype(BF16), b.astype(BF16), (((1,), (1,)), ((), ())),
                           preferred_element_type=F32)


def _bdot_tn(a, b):
    return lax.dot_general(a.astype(BF16), b.astype(BF16), (((0,), (0,)), ((), ())),
                           preferred_element_type=F32)


def _hdot(a, b):
    return jnp.dot(a, b, precision=HIGHEST, preferred_element_type=F32)


def _mod_body(c_ref, w_ref, b_ref, o_ref):
    c = c_ref[...]
    o_ref[...] = _bdot(c * jax.nn.sigmoid(c), w_ref[...]) + b_ref[...]


def _ada_mod(c, ada_w, ada_b):
    bsz, d = c.shape
    n = ada_w.shape[1]
    tn = 1024
    return pl.pallas_call(
        _mod_body,
        grid=(n // tn,),
        in_specs=[pl.BlockSpec((bsz, d), lambda j: (0, 0)),
                  pl.BlockSpec((d, tn), lambda j: (0, j)),
                  pl.BlockSpec((1, tn), lambda j: (0, j))],
        out_specs=pl.BlockSpec((bsz, tn), lambda j: (0, j)),
        out_shape=jax.ShapeDtypeStruct((bsz, n), F32),
        compiler_params=_cparams(("arbitrary",)),
        name="ada_mod",
    )(c, ada_w.astype(BF16), ada_b.reshape(1, n))


def _inproj_body(x_ref, mod_ref, nw_ref, wu_ref, wza_ref, wqkv_ref, wzb_ref, wg_ref,
                 u_ref, za_ref, qkv_ref, zb_ref, g_ref):
    x = x_ref[0]
    ms = jnp.mean(x * x, axis=-1, keepdims=True)
    y = x * lax.rsqrt(ms + NORM_EPS) * nw_ref[...]
    shift = mod_ref[0, 0:1, :]
    scale = mod_ref[0, 1:2, :]
    h = (y * (1.0 + scale) + shift).astype(BF16)
    u_ref[0] = jnp.dot(h, wu_ref[...], preferred_element_type=F32).astype(BF16)
    za_ref[0] = jnp.dot(h, wza_ref[...], preferred_element_type=F32).astype(BF16)
    qkv_ref[0] = jnp.dot(h, wqkv_ref[...], preferred_element_type=F32).astype(BF16)
    zb_ref[0] = jnp.dot(h, wzb_ref[...], preferred_element_type=F32).astype(BF16)
    g_ref[0] = jnp.dot(h, wg_ref[...], preferred_element_type=F32)[:, :g_ref.shape[-1]]


def _in_proj(x, mod3, norm_w, w_in, d_ssm, d_dn, n_gate):
    bsz, seq, d = x.shape
    tm = 512
    o1 = d_ssm
    o2 = o1 + d_ssm
    o3 = o2 + 3 * d_dn
    o4 = o3 + d_dn
    wb = w_in.astype(BF16)
    wu, wza, wqkv, wzb = wb[:, :o1], wb[:, o1:o2], wb[:, o2:o3], wb[:, o3:o4]
    wg = jnp.pad(wb[:, o4:], ((0, 0), (0, 128 - n_gate)))
    full = lambda a: pl.BlockSpec(a.shape, lambda b, i: (0,) * a.ndim)
    row = lambda w: pl.BlockSpec((1, tm, w), lambda b, i: (b, i, 0))
    return pl.pallas_call(
        _inproj_body,
        grid=(bsz, seq // tm),
        in_specs=[row(d), pl.BlockSpec((1, 3, d), lambda b, i: (b, 0, 0)),
                  pl.BlockSpec((1, d), lambda b, i: (0, 0)),
                  full(wu), full(wza), full(wqkv), full(wzb), full(wg)],
        out_specs=[row(d_ssm), row(d_ssm), row(3 * d_dn), row(d_dn), row(n_gate)],
        out_shape=[jax.ShapeDtypeStruct((bsz, seq, d_ssm), BF16),
                   jax.ShapeDtypeStruct((bsz, seq, d_ssm), BF16),
                   jax.ShapeDtypeStruct((bsz, seq, 3 * d_dn), BF16),
                   jax.ShapeDtypeStruct((bsz, seq, d_dn), BF16),
                   jax.ShapeDtypeStruct((bsz, seq, n_gate), F32)],
        compiler_params=_cparams(("arbitrary", "arbitrary")),
        name="in_proj",
    )(x, mod3, norm_w.reshape(1, d), wu, wza, wqkv, wzb, wg)


def _s5_powers(lam_re, lam_im, log_step, b_re, b_im, n_pow):
    dt = jnp.exp(log_step.astype(F32))[:, None]
    lr, li = lam_re.astype(F32), lam_im.astype(F32)
    mag = jnp.exp(lr * dt)
    ab_re, ab_im = mag * jnp.cos(li * dt), mag * jnp.sin(li * dt)
    nr, ni = ab_re - 1.0, ab_im
    den = lr * lr + li * li
    f_re = (nr * lr + ni * li) / den
    f_im = (ni * lr - nr * li) / den
    br, bi = b_re.astype(F32), b_im.astype(F32)
    bb_re = f_re[..., None] * br - f_im[..., None] * bi
    bb_im = f_re[..., None] * bi + f_im[..., None] * br
    tau = jnp.arange(n_pow, dtype=F32)[:, None, None]
    pmag = jnp.exp(tau * (lr * dt)[None])
    ang = tau * (li * dt)[None]
    return pmag * jnp.cos(ang), pmag * jnp.sin(ang), bb_re, bb_im


def _s5_matrices(pf, pb, s5_d, t_len):
    (pfr, pfi, bfr, bfi, cfr, cfi) = pf
    (pbr, pbi, bbr, bbi, cbr, cbi) = pb
    g, n, p = bfr.shape
    t = t_len
    ein = functools.partial(jnp.einsum, precision=HIGHEST)

    def lag_kernels(pr, pi, br, bi, cr, ci):
        xr = pr[:t, :, :, None] * br[None] - pi[:t, :, :, None] * bi[None]
        xi = pr[:t, :, :, None] * bi[None] + pi[:t, :, :, None] * br[None]
        return ein("gpn,tgnq->tgpq", cr, xr) - ein("gpn,tgnq->tgpq", ci, xi)

    kf = lag_kernels(pfr, pfi, bfr, bfi, cfr, cfi)
    kb = lag_kernels(pbr, pbi, bbr, bbi, cbr, cbi)
    dmat = jnp.eye(p, dtype=F32)[None] * s5_d.astype(F32).reshape(g, p)[:, :, None]
    k0 = kf[0] + kb[0] + dmat
    kall = jnp.concatenate([kb[1:][::-1], k0[None], kf[1:]], axis=0)
    idx = jnp.arange(t)[None, :] - jnp.arange(t)[:, None] + (t - 1)
    mi = kall[idx]
    mi = mi.transpose(2, 0, 4, 1, 3).reshape(g, t * p, t * p)

    def in_map(pr, pi, br, bi, order):
        prs, pis = pr[order], pi[order]
        re = prs[..., None] * br[None] - pis[..., None] * bi[None]
        im = prs[..., None] * bi[None] + pis[..., None] * br[None]
        f = lambda a: a.transpose(1, 0, 3, 2).reshape(g, t * p, n)
        return f(re), f(im)

    s_idx = jnp.arange(t)
    bfre, bfim = in_map(pfr, pfi, bfr, bfi, t - 1 - s_idx)
    bbre, bbim = in_map(pbr, pbi, bbr, bbi, s_idx)
    bm = jnp.concatenate([bfre, bbre, bfim, bbim], axis=-1)

    def out_map(pr, pi, cr, ci, order):
        prs, pis = pr[order], pi[order]
        wre = cr[None] * prs[:, :, None, :] - ci[None] * pis[:, :, None, :]
        wim = -(cr[None] * pis[:, :, None, :] + ci[None] * prs[:, :, None, :])
        f = lambda a: a.transpose(1, 3, 0, 2).reshape(g, n, t * p)
        return f(wre), f(wim)

    cfre, cfim = out_map(pfr, pfi, cfr, cfi, s_idx + 1)
    cbre, cbim = out_map(pbr, pbi, cbr, cbi, t - s_idx)
    cm = jnp.concatenate([cfre, cbre, cfim, cbim], axis=1)
    lr = jnp.concatenate([pfr[t], pbr[t]], axis=-1)[:, None, :]
    li = jnp.concatenate([pfi[t], pbi[t]], axis=-1)[:, None, :]
    return mi.astype(BF16), bm.astype(BF16), cm.astype(BF16), lr, li


def _s5_body(u_ref, mi_ref, bm_ref, cm_ref, lr_ref, li_ref, y_ref, con_ref, s0_ref, s1_ref,
             *, gb, nc, bsz):
    n2 = lr_ref.shape[-1]
    for g in range(gb):
        con_ref[g] = jnp.dot(u_ref[g], bm_ref[g], preferred_element_type=F32)
    lane = lax.broadcasted_iota(jnp.int32, (bsz, 2 * n2), 1)
    is_fwd = (lane % n2) < (n2 // 2)
    lrs = [jnp.broadcast_to(lr_ref[g], (bsz, n2)) for g in range(gb)]
    lis = [jnp.broadcast_to(li_ref[g], (bsz, n2)) for g in range(gb)]

    def step(i, carry):
        rf = pl.multiple_of(i * bsz, bsz)
        rb = pl.multiple_of((nc - 1 - i) * bsz, bsz)
        out = []
        for g in range(gb):
            sre, sim = carry[2 * g], carry[2 * g + 1]
            st = jnp.concatenate([sre, sim], axis=-1).astype(BF16)
            s0_ref[g, pl.ds(rf, bsz), :] = st
            s1_ref[g, pl.ds(rb, bsz), :] = st
            c = jnp.where(is_fwd, con_ref[g, pl.ds(rf, bsz), :], con_ref[g, pl.ds(rb, bsz), :])
            nre = lrs[g] * sre - lis[g] * sim + c[:, :n2]
            nim = lrs[g] * sim + lis[g] * sre + c[:, n2:]
            out += [nre, nim]
        return tuple(out)

    zero = jnp.zeros((bsz, n2), F32)
    lax.fori_loop(0, nc, step, (zero,) * (2 * gb))

    rows = nc * bsz
    lane_all = lax.broadcasted_iota(jnp.int32, (rows, 2 * n2), 1)
    sel = (lane_all % n2) < (n2 // 2)
    for g in range(gb):
        xs = jnp.where(sel, s0_ref[g], s1_ref[g])
        y = (jnp.dot(u_ref[g], mi_ref[g], preferred_element_type=F32)
             + jnp.dot(xs, cm_ref[g], preferred_element_type=F32))
        y_ref[g] = jax.nn.gelu(y).astype(BF16)


def _s5_mix(ut, mi, bm, cm, lr, li, nc, bsz):
    g, rows, tp = ut.shape
    gb = 2
    n4 = bm.shape[-1]
    blk = lambda a: pl.BlockSpec((gb,) + a.shape[1:], lambda j: (j,) + (0,) * (a.ndim - 1))
    return pl.pallas_call(
        functools.partial(_s5_body, gb=gb, nc=nc, bsz=bsz),
        grid=(g // gb,),
        in_specs=[blk(ut), blk(mi), blk(bm), blk(cm), blk(lr), blk(li)],
        out_specs=blk(ut),
        out_shape=jax.ShapeDtypeStruct(ut.shape, BF16),
        scratch_shapes=[pltpu.VMEM((gb, rows, n4), F32),
                        pltpu.VMEM((gb, rows, n4), BF16),
                        pltpu.VMEM((gb, rows, n4), BF16)],
        compiler_params=_cparams(("arbitrary",)),
        name="s5_mix",
    )(ut, mi, bm, cm, lr, li)


def _dn_body(q_ref, k_ref, v_ref, cq_ref, ck_ref, cv_ref, gates_ref, gp_ref, o_ref,
             pad_ref, qs_ref, ks_ref, vs_ref, bf_ref, bb_ref, gf_ref, gbk_ref, g16_ref, b16_ref,
             *, seq, ch, nh):
    hd = q_ref.shape[-1]
    h = pl.program_id(1)
    nch = seq // ch
    halo = 8
    rb = 512
    nrb = seq // rb

    pad_ref[pl.ds(0, halo), :] = jnp.zeros((halo, hd), F32)
    pad_ref[pl.ds(halo + seq, halo), :] = jnp.zeros((halo, hd), F32)

    def conv_into(src_ref, w_ref, dst_ref, mode):
        def fill(i, _):
            r0 = pl.multiple_of(i * rb, rb)
            pad_ref[pl.ds(halo + r0, rb), :] = src_ref[0, pl.ds(r0, rb), :].astype(F32)
            return 0
        lax.fori_loop(0, nrb, fill, 0)

        def blk(i, _):
            r0 = pl.multiple_of(i * rb, rb)
            acc = jnp.zeros((rb, hd), F32)
            for j in range(CONV_K):
                acc = acc + pad_ref[pl.ds(halo + r0 + (j - CONV_K // 2), rb), :] * w_ref[j:j + 1, :]
            a = acc * jax.nn.sigmoid(acc)
            if mode == "q":
                a = a * lax.rsqrt(jnp.sum(a * a, axis=-1, keepdims=True) + NORM_EPS) * (hd ** -0.5)
            elif mode == "k":
                a = a * lax.rsqrt(jnp.sum(a * a, axis=-1, keepdims=True) + NORM_EPS)
            dst_ref[pl.ds(r0, rb), :] = a
            return 0
        lax.fori_loop(0, nrb, blk, 0)

    conv_into(q_ref, cq_ref, qs_ref, "q")
    conv_into(k_ref, ck_ref, ks_ref, "k")
    conv_into(v_ref, cv_ref, vs_ref, "v")

    ng = gates_ref.shape[-1]
    gt = gates_ref[0]
    nega = -jnp.exp(gp_ref[0:1, :])
    dtb = gp_ref[1:2, :]
    g16_ref[...] = nega * jax.nn.softplus(gt + dtb)
    b16_ref[...] = jax.nn.sigmoid(gt)

    ii = lax.broadcasted_iota(jnp.int32, (ch, ch), 0)
    jj = lax.broadcasted_iota(jnp.int32, (ch, ch), 1)
    lower = ii >= jj
    upper = ii <= jj
    tril = lower.astype(F32)
    triu = upper.astype(F32)
    sel_r = lax.broadcasted_iota(jnp.int32, (ng, hd), 0)
    oh = lambda col: (sel_r == col).astype(F32)
    oh_bf, oh_bb, oh_gf, oh_gb = oh(h), oh(nh + h), oh(2 * nh + h), oh(3 * nh + h)

    def gate_blk(i, _):
        r0 = pl.multiple_of(i * ch, ch)
        g = g16_ref[pl.ds(r0, ch), :]
        bt = b16_ref[pl.ds(r0, ch), :]
        gf_ref[pl.ds(r0, ch), :] = _hdot(_hdot(tril, g), oh_gf)
        gbk_ref[pl.ds(r0, ch), :] = _hdot(_hdot(triu, g), oh_gb)
        bf_ref[pl.ds(r0, ch), :] = _hdot(bt, oh_bf)
        bb_ref[pl.ds(r0, ch), :] = _hdot(bt, oh_bb)
        return 0
    lax.fori_loop(0, nch, gate_blk, 0)

    o_ref[...] = jnp.zeros(o_ref.shape, F32)
    eye = (ii == jj).astype(F32)
    e0 = (lax.broadcasted_iota(jnp.int32, (ch, hd), 1) == 0).astype(F32)
    same_blk = (ii // DN_SUB) == (jj // DN_SUB)
    n_sub_lvl = int(math.log2(DN_SUB))
    n_blk_lvl = int(math.log2(ch // DN_SUB))

    def chunk(r0, state, beta_ref, gc_ref, mask, strict, last_row):
        qc = qs_ref[pl.ds(r0, ch), :]
        kc = ks_ref[pl.ds(r0, ch), :]
        vc = vs_ref[pl.ds(r0, ch), :]
        beta = beta_ref[pl.ds(r0, ch), :]
        gc = gc_ref[pl.ds(r0, ch), :]
        gc_row = lax.dot_general(e0, gc, (((1,), (1,)), ((), ())), precision=HIGHEST,
                                 preferred_element_type=F32)
        diff = gc[:, :ch] - gc_row
        decay = jnp.where(mask, jnp.exp(jnp.where(mask, diff, 0.0)), 0.0)
        kb = kc * beta
        a = jnp.where(strict, _bdot_nt(kb, kc) * decay, 0.0)
        a_d = jnp.where(same_blk, a, 0.0)
        dinv = eye - a_d
        p = _bdot(a_d, a_d)
        for lvl in range(1, n_sub_lvl):
            dinv = dinv + _bdot(dinv, p)
            if lvl + 1 < n_sub_lvl:
                p = _bdot(p, p)
        bm = _bdot(dinv, a - a_d)
        egc = jnp.exp(gc)
        rhs = jnp.concatenate([vc * beta, kb * egc], axis=-1)
        sol = _bdot(dinv, rhs)
        pw = bm
        for lvl in range(n_blk_lvl):
            sol = sol - _bdot(pw, sol) if lvl == 0 else sol + _bdot(pw, sol)
            if lvl + 1 < n_blk_lvl:
                pw = _bdot(pw, pw)
        u_c, w_c = sol[:, :hd], sol[:, hd:]
        attn = jnp.where(mask, _bdot_nt(qc, kc) * decay, 0.0)
        g_last = gc[last_row:last_row + 1, :]
        kt = kc * jnp.exp(g_last - gc)
        qd = qc * egc
        sb = state.astype(BF16)
        v_new = u_c - _bdot(w_c, sb)
        o = _bdot(qd, sb) + _bdot(attn, v_new)
        state = state * jnp.exp(g_last) + _bdot_tn(kt, v_new)
        o_ref[0, pl.ds(r0, ch), :] += o
        return state

    def step(i, carry):
        sf, sb = carry
        rf = pl.multiple_of(i * ch, ch)
        rbk = pl.multiple_of((nch - 1 - i) * ch, ch)
        sf = chunk(rf, sf, bf_ref, gf_ref, lower, ii > jj, ch - 1)
        sb = chunk(rbk, sb, bb_ref, gbk_ref, upper, ii < jj, 0)
        return sf, sb

    z = jnp.zeros((hd, hd), F32)
    lax.fori_loop(0, nch, step, (z, z))


def _delta_net(qkv, conv_w, gates, gparams, nh):
    bsz, seq, _ = qkv.shape
    hd = DN_HEAD_DIM
    ng = gates.shape[-1]
    col = lambda off: pl.BlockSpec((1, seq, hd), lambda b, h: (b, 0, off + h))
    cw = lambda off: pl.BlockSpec((CONV_K, hd), lambda b, h: (0, off + h))
    big = pltpu.VMEM((seq, hd), F32)
    return pl.pallas_call(
        functools.partial(_dn_body, seq=seq, ch=DN_CHUNK, nh=nh),
        grid=(bsz, nh),
        in_specs=[col(0), col(nh), col(2 * nh), cw(0), cw(nh), cw(2 * nh),
                  pl.BlockSpec((1, seq, ng), lambda b, h: (b, 0, 0)),
                  pl.BlockSpec((2, ng), lambda b, h: (0, 0))],
        out_specs=pl.BlockSpec((1, seq, hd), lambda b, h: (b, 0, h)),
        out_shape=jax.ShapeDtypeStruct((bsz, seq, nh * hd), F32),
        scratch_shapes=[pltpu.VMEM((seq + 16, hd), F32), big, big, big, big, big, big, big,
                        pltpu.VMEM((seq, ng), F32), pltpu.VMEM((seq, ng), F32)],
        compiler_params=_cparams(("arbitrary", "arbitrary")),
        name="delta_net",
    )(qkv, qkv, qkv, conv_w, conv_w, conv_w, gates, gparams)


def _out_body(x_ref, mod_ref, ys_ref, za_ref, o_ref, zb_ref, gw_ref, gbias_ref, dnw_ref,
              wo_ref, fw_ref, out_ref, *, nh):
    d_ssm = ys_ref.shape[-1]
    hd = o_ref.shape[-1] // nh
    glu = jnp.dot(ys_ref[0], gw_ref[...], preferred_element_type=F32) + gbias_ref[...]
    za = za_ref[0].astype(F32)
    ya = glu[:, :d_ssm] * jax.nn.sigmoid(glu[:, d_ssm:]) * (za * jax.nn.sigmoid(za))
    mix = jnp.dot(ya.astype(BF16), wo_ref[pl.ds(0, d_ssm), :], preferred_element_type=F32)
    for hh in range(nh):
        o = o_ref[0, :, hh * hd:(hh + 1) * hd]
        zb = zb_ref[0, :, hh * hd:(hh + 1) * hd].astype(F32)
        o = o * lax.rsqrt(jnp.mean(o * o, axis=-1, keepdims=True) + NORM_EPS) * dnw_ref[...]
        yb = (o * (zb * jax.nn.sigmoid(zb))).astype(BF16)
        mix = mix + jnp.dot(yb, wo_ref[pl.ds(d_ssm + hh * hd, hd), :], preferred_element_type=F32)
    gate = mod_ref[0, 2:3, :]
    r = x_ref[0] + gate * mix
    out_ref[0] = r * lax.rsqrt(jnp.mean(r * r, axis=-1, keepdims=True) + NORM_EPS) * fw_ref[...]


def _out_proj(x, mod3, ys, za, o, zb, glu_w, glu_b, dn_norm_w, w_out, final_norm_w, nh):
    bsz, seq, d = x.shape
    tm = 512
    full = lambda a: pl.BlockSpec(a.shape, lambda b, i: (0,) * a.ndim)
    row = lambda a: pl.BlockSpec((1, tm, a.shape[-1]), lambda b, i: (b, i, 0))
    gw = glu_w.astype(BF16)
    gbias = glu_b.reshape(1, -1)
    dnw = dn_norm_w.reshape(1, -1)
    wo = w_out.astype(BF16)
    fw = final_norm_w.reshape(1, -1)
    return pl.pallas_call(
        functools.partial(_out_body, nh=nh),
        grid=(bsz, seq // tm),
        in_specs=[row(x), pl.BlockSpec((1, 3, d), lambda b, i: (b, 0, 0)),
                  row(ys), row(za), row(o), row(zb),
                  full(gw), full(gbias), full(dnw), full(wo), full(fw)],
        out_specs=row(x),
        out_shape=jax.ShapeDtypeStruct(x.shape, F32),
        compiler_params=_cparams(("arbitrary", "arbitrary")),
        name="out_proj",
    )(x, mod3, ys, za, o, zb, gw, gbias, dnw, wo, fw)


def _layer(x, c, ada_w, ada_b, norm_w, w_in, conv_w,
           dn_a_log_f, dn_dt_bias_f, dn_a_log_b, dn_dt_bias_b, dn_norm_w,
           s5f, s5b, s5_d, glu_w, glu_b, w_out, out_norm_w):
    bsz, seq, d = x.shape
    d_ssm = s5_d.shape[0]
    d_dn = w_out.shape[0] - d_ssm
    nh = d_dn // DN_HEAD_DIM
    n_gate = 4 * nh
    p = SSM_GROUP
    g = d_ssm // p
    t = S5_T
    nc = seq // t

    mod3 = _ada_mod(c, ada_w, ada_b).reshape(bsz, 3, d)
    u, za, qkv, zb, gates = _in_proj(x, mod3, norm_w, w_in, d_ssm, d_dn, n_gate)

    def mats(prm):
        lam_re, lam_im, log_step, b_re, b_im, c_re, c_im = prm
        pr, pi, br, bi = _s5_powers(lam_re, lam_im, log_step, b_re, b_im, t + 1)
        return pr, pi, br, bi, c_re.astype(F32), c_im.astype(F32)
    mi, bm, cm, lr, li = _s5_matrices(mats(s5f), mats(s5b), s5_d, t)
    ut = u.reshape(bsz, nc, t, g, p).transpose(3, 1, 0, 2, 4).reshape(g, nc * bsz, t * p)
    yt = _s5_mix(ut, mi, bm, cm, lr, li, nc, bsz)
    ys = yt.reshape(g, nc, bsz, t, p).transpose(2, 1, 3, 0, 4).reshape(bsz, seq, d_ssm)

    zeros = jnp.zeros((2 * nh,), F32)
    gparams = jnp.stack([
        jnp.concatenate([zeros, dn_a_log_f.astype(F32), dn_a_log_b.astype(F32)]),
        jnp.concatenate([zeros, dn_dt_bias_f.astype(F32), dn_dt_bias_b.astype(F32)])])
    o = _delta_net(qkv, conv_w.astype(F32), gates, gparams, nh)

    return _out_proj(x, mod3, ys, za, o, zb, glu_w, glu_b, dn_norm_w, w_out, out_norm_w, nh)


def kernel(x, c, ada_w, ada_b, norm_w, w_in, conv_w, dn_a_log_f, dn_dt_bias_f, dn_a_log_b,
           dn_dt_bias_b, dn_norm_w, lam_re_f, lam_im_f, log_step_f, b_re_f, b_im_f, c_re_f, c_im_f,
           lam_re_b, lam_im_b, log_step_b, b_re_b, b_im_b, c_re_b, c_im_b, s5_d, glu_w, glu_b,
           w_out, final_norm_w):
    assert ada_w.shape[0] == 1, "single-layer block"
    s5f = (lam_re_f[0], lam_im_f[0], log_step_f[0], b_re_f[0], b_im_f[0], c_re_f[0], c_im_f[0])
    s5b = (lam_re_b[0], lam_im_b[0], log_step_b[0], b_re_b[0], b_im_b[0], c_re_b[0], c_im_b[0])
    return _layer(x, c, ada_w[0], ada_b[0], norm_w[0], w_in[0], conv_w[0],
                  dn_a_log_f[0], dn_dt_bias_f[0], dn_a_log_b[0], dn_dt_bias_b[0], dn_norm_w[0],
                  s5f, s5b, s5_d[0], glu_w[0], glu_b[0], w_out[0], final_norm_w)
```

```python
import functools
import math

import jax
import jax.numpy as jnp
from jax import lax
from jax.experimental import pallas as pl
from jax.experimental.pallas import tpu as pltpu

F32 = jnp.float32
BF16 = jnp.bfloat16
HIGHEST = lax.Precision.HIGHEST

NORM_EPS = 1e-6
SSM_GROUP = 16
SSM_STATE = 64
DN_HEAD_DIM = 128
CONV_K = 5
S5_T = 16
DN_CHUNK = 64
DN_SUB = 16
VMEM_LIMIT = 56 * 1024 * 1024


def _cparams(sem):
    return pltpu.CompilerParams(dimension_semantics=sem, vmem_limit_bytes=VMEM_LIMIT)


def _bdot(a, b):
    return jnp.dot(a.astype(BF16), b.astype(BF16), preferred_element_type=F32)


def _bdot_nt(a, b):
    return lax.dot_general(a.astype(BF16), b.astype(BF16), (((1,), (1,)), ((), ())),
                           preferred_element_type=F32)


def _bdot_tn(a, b):
    return lax.dot_general(a.astype(BF16), b.astype(BF16), (((0,), (0,)), ((), ())),
                           preferred_element_type=F32)


def _split3(x):
    hi = x.astype(BF16)
    r1 = x - hi.astype(F32)
    mid = r1.astype(BF16)
    lo = (r1 - mid.astype(F32)).astype(BF16)
    return hi, mid, lo


def _mod_body(c_ref, w_ref, b_ref, o_ref):
    c = c_ref[...]
    o_ref[...] = _bdot(c * jax.nn.sigmoid(c), w_ref[...]) + b_ref[...]


def _ada_mod(c, ada_w, ada_b):
    bsz, d = c.shape
    n = ada_w.shape[1]
    tn = 1024
    return pl.pallas_call(
        _mod_body,
        grid=(n // tn,),
        in_specs=[pl.BlockSpec((bsz, d), lambda j: (0, 0)),
                  pl.BlockSpec((d, tn), lambda j: (0, j)),
                  pl.BlockSpec((1, tn), lambda j: (0, j))],
        out_specs=pl.BlockSpec((bsz, tn), lambda j: (0, j)),
        out_shape=jax.ShapeDtypeStruct((bsz, n), F32),
        compiler_params=_cparams(("arbitrary",)),
        name="ada_mod",
    )(c, ada_w.astype(BF16), ada_b.reshape(1, n))


def _inproj_body(x_ref, mod_ref, nw_ref, wu_ref, wza_ref, wqkv_ref, wzb_ref, wg_ref,
                 wgt_ref, u_ref, za_ref, qkv_ref, zb_ref, g_ref, gt_ref):
    x = x_ref[0]
    ms = jnp.mean(x * x, axis=-1, keepdims=True)
    y = x * lax.rsqrt(ms + NORM_EPS) * nw_ref[...]
    shift = mod_ref[0, 0:1, :]
    scale = mod_ref[0, 1:2, :]
    h = (y * (1.0 + scale) + shift).astype(BF16)
    u_ref[0] = jnp.dot(h, wu_ref[...], preferred_element_type=F32).astype(BF16)
    za_ref[0] = jnp.dot(h, wza_ref[...], preferred_element_type=F32).astype(BF16)
    qkv_ref[0] = jnp.dot(h, wqkv_ref[...], preferred_element_type=F32).astype(BF16)
    zb_ref[0] = jnp.dot(h, wzb_ref[...], preferred_element_type=F32).astype(BF16)
    g_ref[0] = jnp.dot(h, wg_ref[...], preferred_element_type=F32)[:, :g_ref.shape[-1]]
    gt_ref[0] = lax.dot_general(wgt_ref[...], h, (((1,), (1,)), ((), ())), preferred_element_type=F32)


def _in_proj(x, mod3, norm_w, w_in, d_ssm, d_dn, n_gate):
    bsz, seq, d = x.shape
    tm = 512
    o1 = d_ssm
    o2 = o1 + d_ssm
    o3 = o2 + 3 * d_dn
    o4 = o3 + d_dn
    wb = w_in.astype(BF16)
    wu, wza, wqkv, wzb = wb[:, :o1], wb[:, o1:o2], wb[:, o2:o3], wb[:, o3:o4]
    wg = jnp.pad(wb[:, o4:], ((0, 0), (0, 128 - n_gate)))
    wgt = wb[:, o4:].T
    full = lambda a: pl.BlockSpec(a.shape, lambda b, i: (0,) * a.ndim)
    row = lambda w: pl.BlockSpec((1, tm, w), lambda b, i: (b, i, 0))
    return pl.pallas_call(
        _inproj_body,
        grid=(bsz, seq // tm),
        in_specs=[row(d), pl.BlockSpec((1, 3, d), lambda b, i: (b, 0, 0)),
                  pl.BlockSpec((1, d), lambda b, i: (0, 0)),
                  full(wu), full(wza), full(wqkv), full(wzb), full(wg), full(wgt)],
        out_specs=[row(d_ssm), row(d_ssm), row(3 * d_dn), row(d_dn), row(n_gate),
                   pl.BlockSpec((1, n_gate, tm), lambda b, i: (b, 0, i))],
        out_shape=[jax.ShapeDtypeStruct((bsz, seq, d_ssm), BF16),
                   jax.ShapeDtypeStruct((bsz, seq, d_ssm), BF16),
                   jax.ShapeDtypeStruct((bsz, seq, 3 * d_dn), BF16),
                   jax.ShapeDtypeStruct((bsz, seq, d_dn), BF16),
                   jax.ShapeDtypeStruct((bsz, seq, n_gate), F32),
                   jax.ShapeDtypeStruct((bsz, n_gate, seq), F32)],
        compiler_params=_cparams(("arbitrary", "arbitrary")),
        name="in_proj",
    )(x, mod3, norm_w.reshape(1, d), wu, wza, wqkv, wzb, wg, wgt)


def _s5_powers(lam_re, lam_im, log_step, b_re, b_im, n_pow):
    dt = jnp.exp(log_step.astype(F32))[:, None]
    lr, li = lam_re.astype(F32), lam_im.astype(F32)
    mag = jnp.exp(lr * dt)
    ab_re, ab_im = mag * jnp.cos(li * dt), mag * jnp.sin(li * dt)
    nr, ni = ab_re - 1.0, ab_im
    den = lr * lr + li * li
    f_re = (nr * lr + ni * li) / den
    f_im = (ni * lr - nr * li) / den
    br, bi = b_re.astype(F32), b_im.astype(F32)
    bb_re = f_re[..., None] * br - f_im[..., None] * bi
    bb_im = f_re[..., None] * bi + f_im[..., None] * br
    tau = jnp.arange(n_pow, dtype=F32)[:, None, None]
    pmag = jnp.exp(tau * (lr * dt)[None])
    ang = tau * (li * dt)[None]
    return pmag * jnp.cos(ang), pmag * jnp.sin(ang), bb_re, bb_im


def _s5_matrices(pf, pb, s5_d, t_len):
    (pfr, pfi, bfr, bfi, cfr, cfi) = pf
    (pbr, pbi, bbr, bbi, cbr, cbi) = pb
    g, n, p = bfr.shape
    t = t_len
    ein = functools.partial(jnp.einsum, precision=HIGHEST)

    def lag_kernels(pr, pi, br, bi, cr, ci):
        xr = pr[:t, :, :, None] * br[None] - pi[:t, :, :, None] * bi[None]
        xi = pr[:t, :, :, None] * bi[None] + pi[:t, :, :, None] * br[None]
        return ein("gpn,tgnq->tgpq", cr, xr) - ein("gpn,tgnq->tgpq", ci, xi)

    kf = lag_kernels(pfr, pfi, bfr, bfi, cfr, cfi)
    kb = lag_kernels(pbr, pbi, bbr, bbi, cbr, cbi)
    dmat = jnp.eye(p, dtype=F32)[None] * s5_d.astype(F32).reshape(g, p)[:, :, None]
    k0 = kf[0] + kb[0] + dmat
    kall = jnp.concatenate([kb[1:][::-1], k0[None], kf[1:]], axis=0)
    idx = jnp.arange(t)[None, :] - jnp.arange(t)[:, None] + (t - 1)
    mi = kall[idx]
    mi = mi.transpose(2, 0, 4, 1, 3).reshape(g, t * p, t * p)

    def in_map(pr, pi, br, bi, order):
        prs, pis = pr[order], pi[order]
        re = prs[..., None] * br[None] - pis[..., None] * bi[None]
        im = prs[..., None] * bi[None] + pis[..., None] * br[None]
        f = lambda a: a.transpose(1, 0, 3, 2).reshape(g, t * p, n)
        return f(re), f(im)

    s_idx = jnp.arange(t)
    bfre, bfim = in_map(pfr, pfi, bfr, bfi, t - 1 - s_idx)
    bbre, bbim = in_map(pbr, pbi, bbr, bbi, s_idx)
    bm = jnp.concatenate([bfre, bbre, bfim, bbim], axis=-1)

    def out_map(pr, pi, cr, ci, order):
        prs, pis = pr[order], pi[order]
        wre = cr[None] * prs[:, :, None, :] - ci[None] * pis[:, :, None, :]
        wim = -(cr[None] * pis[:, :, None, :] + ci[None] * prs[:, :, None, :])
        f = lambda a: a.transpose(1, 3, 0, 2).reshape(g, n, t * p)
        return f(wre), f(wim)

    cfre, cfim = out_map(pfr, pfi, cfr, cfi, s_idx + 1)
    cbre, cbim = out_map(pbr, pbi, cbr, cbi, t - s_idx)
    cm = jnp.concatenate([cfre, cbre, cfim, cbim], axis=1)
    lr = jnp.concatenate([pfr[t], pbr[t]], axis=-1)[:, None, :]
    li = jnp.concatenate([pfi[t], pbi[t]], axis=-1)[:, None, :]
    return mi.astype(BF16), bm.astype(BF16), cm.astype(BF16), lr, li


def _s5_body(u_ref, mi_ref, bm_ref, cm_ref, lr_ref, li_ref, y_ref, con_ref, s0_ref, s1_ref,
             *, gb, nc, bsz):
    n2 = lr_ref.shape[-1]
    for g in range(gb):
        con_ref[g] = jnp.dot(u_ref[g], bm_ref[g], preferred_element_type=F32)
    lane = lax.broadcasted_iota(jnp.int32, (bsz, 2 * n2), 1)
    is_fwd = (lane % n2) < (n2 // 2)
    lrs = [jnp.broadcast_to(lr_ref[g], (bsz, n2)) for g in range(gb)]
    lis = [jnp.broadcast_to(li_ref[g], (bsz, n2)) for g in range(gb)]

    def step(i, carry):
        rf = pl.multiple_of(i * bsz, bsz)
        rb = pl.multiple_of((nc - 1 - i) * bsz, bsz)
        out = []
        for g in range(gb):
            sre, sim = carry[2 * g], carry[2 * g + 1]
            st = jnp.concatenate([sre, sim], axis=-1).astype(BF16)
            s0_ref[g, pl.ds(rf, bsz), :] = st
            s1_ref[g, pl.ds(rb, bsz), :] = st
            c = jnp.where(is_fwd, con_ref[g, pl.ds(rf, bsz), :], con_ref[g, pl.ds(rb, bsz), :])
            nre = lrs[g] * sre - lis[g] * sim + c[:, :n2]
            nim = lrs[g] * sim + lis[g] * sre + c[:, n2:]
            out += [nre, nim]
        return tuple(out)

    zero = jnp.zeros((bsz, n2), F32)
    lax.fori_loop(0, nc, step, (zero,) * (2 * gb))

    rows = nc * bsz
    lane_all = lax.broadcasted_iota(jnp.int32, (rows, 2 * n2), 1)
    sel = (lane_all % n2) < (n2 // 2)
    for g in range(gb):
        xs = jnp.where(sel, s0_ref[g], s1_ref[g])
        y = (jnp.dot(u_ref[g], mi_ref[g], preferred_element_type=F32)
             + jnp.dot(xs, cm_ref[g], preferred_element_type=F32))
        y_ref[g] = jax.nn.gelu(y).astype(BF16)


def _s5_mix(ut, mi, bm, cm, lr, li, nc, bsz):
    g, rows, tp = ut.shape
    gb = 2
    n4 = bm.shape[-1]
    blk = lambda a: pl.BlockSpec((gb,) + a.shape[1:], lambda j: (j,) + (0,) * (a.ndim - 1))
    return pl.pallas_call(
        functools.partial(_s5_body, gb=gb, nc=nc, bsz=bsz),
        grid=(g // gb,),
        in_specs=[blk(ut), blk(mi), blk(bm), blk(cm), blk(lr), blk(li)],
        out_specs=blk(ut),
        out_shape=jax.ShapeDtypeStruct(ut.shape, BF16),
        scratch_shapes=[pltpu.VMEM((gb, rows, n4), F32),
                        pltpu.VMEM((gb, rows, n4), BF16),
                        pltpu.VMEM((gb, rows, n4), BF16)],
        compiler_params=_cparams(("arbitrary",)),
        name="s5_mix",
    )(ut, mi, bm, cm, lr, li)


def _dn_body(q_ref, k_ref, v_ref, cq_ref, ck_ref, cv_ref, gates_ref, gt_ref, gp_ref, o_ref,
             pad_ref, qs_ref, ks_ref, vs_ref, bcol_ref, gcol_ref, grow_ref,
             mq_ref, n_ref, op_ref, el_ref, *, seq, ch, nh):
    hd = q_ref.shape[-1]
    h = pl.program_id(1)
    nch = seq // ch
    halo = 8
    rb = 512
    nrb = seq // rb
    ng = gates_ref.shape[-1]

    pad_ref[pl.ds(0, halo), :] = jnp.zeros((halo, hd), F32)
    pad_ref[pl.ds(halo + seq, halo), :] = jnp.zeros((halo, hd), F32)

    def conv_into(src_ref, w_ref, dst_ref, mode):
        def fill(i, _):
            r0 = pl.multiple_of(i * rb, rb)
            pad_ref[pl.ds(halo + r0, rb), :] = src_ref[0, pl.ds(r0, rb), :].astype(F32)
            return 0
        lax.fori_loop(0, nrb, fill, 0)

        def blk(i, _):
            r0 = pl.multiple_of(i * rb, rb)
            acc = jnp.zeros((rb, hd), F32)
            for j in range(CONV_K):
                acc = acc + pad_ref[pl.ds(halo + r0 + (j - CONV_K // 2), rb), :] * w_ref[j:j + 1, :]
            a = acc * jax.nn.sigmoid(acc)
            if mode == "q":
                a = a * lax.rsqrt(jnp.sum(a * a, axis=-1, keepdims=True) + NORM_EPS) * (hd ** -0.5)
            elif mode == "k":
                a = a * lax.rsqrt(jnp.sum(a * a, axis=-1, keepdims=True) + NORM_EPS)
            dst_ref[pl.ds(r0, rb), :] = a.astype(dst_ref.dtype)
            return 0
        lax.fori_loop(0, nrb, blk, 0)

    conv_into(q_ref, cq_ref, qs_ref, "q")
    conv_into(k_ref, ck_ref, ks_ref, "k")
    conv_into(v_ref, cv_ref, vs_ref, "v")

    nega = -jnp.exp(gp_ref[0:1, :])
    dtb = gp_ref[1:2, :]
    cols = (h, nh + h, 2 * nh + h, 3 * nh + h)
    gb = 256
    bi = lax.broadcasted_iota(jnp.int32, (gb, gb), 0)
    bj = lax.broadcasted_iota(jnp.int32, (gb, gb), 1)
    same_chunk = (bi // ch) == (bj // ch)
    pre = (same_chunk & (bi >= bj)).astype(BF16)
    suf = (same_chunk & (bi <= bj)).astype(BF16)
    sel_c = lax.broadcasted_iota(jnp.int32, (gb, ng), 1)
    sel_l = lax.broadcasted_iota(jnp.int32, (1, ng), 1)

    def col_of(x, col):
        return jnp.broadcast_to(jnp.sum(jnp.where(sel_c == col, x, 0.0), axis=-1, keepdims=True),
                                (gb, hd))

    def gate_blk(i, _):
        r0 = pl.multiple_of(i * gb, gb)
        gt = gates_ref[0, pl.ds(r0, gb), :]
        parts = _split3(nega * jax.nn.softplus(gt + dtb))
        b16 = jax.nn.sigmoid(gt)
        for d in range(2):
            tri = pre if d == 0 else suf
            cs = sum(jnp.dot(tri, p, preferred_element_type=F32) for p in parts)
            gcol_ref[d, pl.ds(r0, gb), :] = col_of(cs, cols[2 + d])
            bcol_ref[d, pl.ds(r0, gb), :] = col_of(b16, cols[d])
        return 0
    lax.fori_loop(0, seq // gb, gate_blk, 0)

    ii = lax.broadcasted_iota(jnp.int32, (ch, ch), 0)
    jj = lax.broadcasted_iota(jnp.int32, (ch, ch), 1)
    for d in range(2):
        pick = sel_l == cols[2 + d]
        na = jnp.sum(jnp.where(pick, nega, 0.0), axis=-1, keepdims=True)
        db = jnp.sum(jnp.where(pick, dtb, 0.0), axis=-1, keepdims=True)
        parts = _split3(na * jax.nn.softplus(gt_ref[0, cols[2 + d]] + db))
        tri = ((ii <= jj) if d == 0 else (ii >= jj)).astype(BF16)
        grow_ref[d] = sum(jnp.dot(p, tri, preferred_element_type=F32) for p in parts)

    masks = (ii >= jj, ii <= jj)
    stricts = (ii > jj, ii < jj)
    last_rows = (ch - 1, 0)
    eye = (ii == jj).astype(F32)
    same_blk = (ii // DN_SUB) == (jj // DN_SUB)
    n_sub_lvl = int(math.log2(DN_SUB))
    n_blk_lvl = int(math.log2(ch // DN_SUB))

    def prep_load(c):
        r0 = pl.multiple_of(c * ch, ch)
        vals = [qs_ref[pl.ds(r0, ch), :], ks_ref[pl.ds(r0, ch), :], vs_ref[pl.ds(r0, ch), :]]
        for d in range(2):
            vals += [bcol_ref[d, pl.ds(r0, ch), :], gcol_ref[d, pl.ds(r0, ch), :],
                     grow_ref[d, pl.ds(c, 1), :]]
        return vals

    def prep_compute(loaded):
        chains = []
        for vals in loaded:
            qc, kc, vc = vals[0], vals[1], vals[2].astype(F32)
            kk = _bdot_nt(kc, kc)
            qk = _bdot_nt(qc, kc)
            kf, qf = kc.astype(F32), qc.astype(F32)
            for d in range(2):
                beta, gc, gr = vals[3 + 3 * d:6 + 3 * d]
                mask = masks[d]
                decay = jnp.where(mask, jnp.exp(jnp.where(mask, gc[:, :ch] - gr, 0.0)), 0.0)
                a = jnp.where(stricts[d], beta[:, :ch] * kk * decay, 0.0)
                a_d = jnp.where(same_blk, a, 0.0)
                egc = jnp.exp(gc)
                g_last = gc[last_rows[d]:last_rows[d] + 1, :]
                chains.append(dict(
                    a_d=a_d, dinv=eye - a_d,
                    rhs=jnp.concatenate([kf * (beta * egc), vc * beta, a - a_d], axis=-1),
                    attn=jnp.where(mask, qk * decay, 0.0), qd=qf * egc,
                    kt=kf * jnp.exp(g_last - gc), el=jnp.exp(g_last)))
        for cn in chains:
            cn["p"] = _bdot(cn["a_d"], cn["a_d"])
        for lvl in range(1, n_sub_lvl):
            for cn in chains:
                if lvl + 1 < n_sub_lvl:
                    both = _bdot(jnp.concatenate([cn["dinv"], cn["p"]], axis=0), cn["p"])
                    cn["dinv"] = cn["dinv"] + both[:ch]
                    cn["p"] = both[ch:]
                else:
                    cn["dinv"] = cn["dinv"] + _bdot(cn["dinv"], cn["p"])
        for cn in chains:
            first = _bdot(cn["dinv"], cn["rhs"])
            cn["sol"] = first[:, :2 * hd]
            cn["pw"] = first[:, 2 * hd:]
        for lvl in range(n_blk_lvl):
            for cn in chains:
                upd = _bdot(cn["pw"], cn["sol"])
                cn["sol"] = cn["sol"] - upd if lvl == 0 else cn["sol"] + upd
                if lvl + 1 < n_blk_lvl:
                    cn["pw"] = _bdot(cn["pw"], cn["pw"])
        outs = []
        for cn in chains:
            cn["x1"] = _bdot_tn(cn["kt"], cn["sol"])
        for cn in chains:
            cn["x2"] = _bdot(cn["attn"], cn["sol"])
        for cn in chains:
            x1, x2 = cn["x1"], cn["x2"]
            outs.append((jnp.concatenate([-x1[:, :hd], cn["qd"] - x2[:, :hd]], axis=0).astype(BF16),
                         x1[:, hd:], x2[:, hd:], cn["el"]))
        return [outs[2 * j:2 * j + 2] for j in range(len(loaded))]

    def prep_store(c, outs):
        r0 = pl.multiple_of(c * ch, ch)
        for d in range(2):
            mq_ref[d, c] = outs[d][0]
            n_ref[d, c] = outs[d][1]
            op_ref[d, pl.ds(r0, ch), :] = outs[d][2]
            el_ref[d, pl.ds(c, 1), :] = outs[d][3]

    na_par = 8

    def prep_step(i, _):
        loaded = [prep_load(i * na_par + j) for j in range(na_par)]
        outs = prep_compute(loaded)
        for j in range(na_par):
            prep_store(i * na_par + j, outs[j])
        return 0
    lax.fori_loop(0, nch // na_par, prep_step, 0)

    o_ref[...] = jnp.zeros(o_ref.shape, F32)

    def state_step(i, carry):
        new = []
        for d in range(2):
            state = carry[d]
            c = i if d == 0 else nch - 1 - i
            r0 = pl.multiple_of(c * ch, ch)
            r = jnp.dot(mq_ref[d, c], state.astype(BF16), preferred_element_type=F32)
            o_ref[0, pl.ds(r0, ch), :] += r[hd:] + op_ref[d, pl.ds(r0, ch), :]
            new.append(state * el_ref[d, pl.ds(c, 1), :] + r[:hd] + n_ref[d, c])
        return tuple(new)

    z = jnp.zeros((hd, hd), F32)
    lax.fori_loop(0, nch, state_step, (z, z))


def _delta_net(qkv, conv_w, gates, gates_t, gparams, nh):
    bsz, seq, _ = qkv.shape
    hd = DN_HEAD_DIM
    ch = DN_CHUNK
    nch = seq // ch
    ng = gates.shape[-1]
    col = lambda off: pl.BlockSpec((1, seq, hd), lambda b, h: (b, 0, off + h))
    cw = lambda off: pl.BlockSpec((CONV_K, hd), lambda b, h: (0, off + h))
    return pl.pallas_call(
        functools.partial(_dn_body, seq=seq, ch=ch, nh=nh),
        grid=(bsz, nh),
        in_specs=[col(0), col(nh), col(2 * nh), cw(0), cw(nh), cw(2 * nh),
                  pl.BlockSpec((1, seq, ng), lambda b, h: (b, 0, 0)),
                  pl.BlockSpec((1, ng, nch, ch), lambda b, h: (b, 0, 0, 0)),
                  pl.BlockSpec((2, ng), lambda b, h: (0, 0))],
        out_specs=pl.BlockSpec((1, seq, hd), lambda b, h: (b, 0, h)),
        out_shape=jax.ShapeDtypeStruct((bsz, seq, nh * hd), F32),
        scratch_shapes=[pltpu.VMEM((seq + 16, hd), F32),
                        pltpu.VMEM((seq, hd), BF16),
                        pltpu.VMEM((seq, hd), BF16),
                        pltpu.VMEM((seq, hd), BF16),
                        pltpu.VMEM((2, seq, hd), F32),
                        pltpu.VMEM((2, seq, hd), F32),
                        pltpu.VMEM((2, nch, ch), F32),
                        pltpu.VMEM((2, nch, hd + ch, hd), BF16),
                        pltpu.VMEM((2, nch, hd, hd), F32),
                        pltpu.VMEM((2, seq, hd), F32),
                        pltpu.VMEM((2, nch, hd), F32)],
        compiler_params=_cparams(("arbitrary", "arbitrary")),
        name="delta_net",
    )(qkv, qkv, qkv, conv_w, conv_w, conv_w, gates, gates_t, gparams)


def _out_body(x_ref, mod_ref, ys_ref, za_ref, o_ref, zb_ref, gw_ref, gbias_ref, dnw_ref,
              wo_ref, fw_ref, out_ref, *, nh):
    d_ssm = ys_ref.shape[-1]
    hd = o_ref.shape[-1] // nh
    glu = jnp.dot(ys_ref[0], gw_ref[...], preferred_element_type=F32) + gbias_ref[...]
    za = za_ref[0].astype(F32)
    ya = glu[:, :d_ssm] * jax.nn.sigmoid(glu[:, d_ssm:]) * (za * jax.nn.sigmoid(za))
    mix = jnp.dot(ya.astype(BF16), wo_ref[pl.ds(0, d_ssm), :], preferred_element_type=F32)
    for hh in range(nh):
        o = o_ref[0, :, hh * hd:(hh + 1) * hd]
        zb = zb_ref[0, :, hh * hd:(hh + 1) * hd].astype(F32)
        o = o * lax.rsqrt(jnp.mean(o * o, axis=-1, keepdims=True) + NORM_EPS) * dnw_ref[...]
        yb = (o * (zb * jax.nn.sigmoid(zb))).astype(BF16)
        mix = mix + jnp.dot(yb, wo_ref[pl.ds(d_ssm + hh * hd, hd), :], preferred_element_type=F32)
    gate = mod_ref[0, 2:3, :]
    r = x_ref[0] + gate * mix
    out_ref[0] = r * lax.rsqrt(jnp.mean(r * r, axis=-1, keepdims=True) + NORM_EPS) * fw_ref[...]


def _out_proj(x, mod3, ys, za, o, zb, glu_w, glu_b, dn_norm_w, w_out, final_norm_w, nh):
    bsz, seq, d = x.shape
    tm = 512
    full = lambda a: pl.BlockSpec(a.shape, lambda b, i: (0,) * a.ndim)
    row = lambda a: pl.BlockSpec((1, tm, a.shape[-1]), lambda b, i: (b, i, 0))
    gw = glu_w.astype(BF16)
    gbias = glu_b.reshape(1, -1)
    dnw = dn_norm_w.reshape(1, -1)
    wo = w_out.astype(BF16)
    fw = final_norm_w.reshape(1, -1)
    return pl.pallas_call(
        functools.partial(_out_body, nh=nh),
        grid=(bsz, seq // tm),
        in_specs=[row(x), pl.BlockSpec((1, 3, d), lambda b, i: (b, 0, 0)),
                  row(ys), row(za), row(o), row(zb),
                  full(gw), full(gbias), full(dnw), full(wo), full(fw)],
        out_specs=row(x),
        out_shape=jax.ShapeDtypeStruct(x.shape, F32),
        compiler_params=_cparams(("arbitrary", "arbitrary")),
        name="out_proj",
    )(x, mod3, ys, za, o, zb, gw, gbias, dnw, wo, fw)


def _layer(x, c, ada_w, ada_b, norm_w, w_in, conv_w,
           dn_a_log_f, dn_dt_bias_f, dn_a_log_b, dn_dt_bias_b, dn_norm_w,
           s5f, s5b, s5_d, glu_w, glu_b, w_out, out_norm_w):
    bsz, seq, d = x.shape
    d_ssm = s5_d.shape[0]
    d_dn = w_out.shape[0] - d_ssm
    nh = d_dn // DN_HEAD_DIM
    n_gate = 4 * nh
    p = SSM_GROUP
    g = d_ssm // p
    t = S5_T
    nc = seq // t

    mod3 = _ada_mod(c, ada_w, ada_b).reshape(bsz, 3, d)
    u, za, qkv, zb, gates, gates_t = _in_proj(x, mod3, norm_w, w_in, d_ssm, d_dn, n_gate)

    def mats(prm):
        lam_re, lam_im, log_step, b_re, b_im, c_re, c_im = prm
        pr, pi, br, bi = _s5_powers(lam_re, lam_im, log_step, b_re, b_im, t + 1)
        return pr, pi, br, bi, c_re.astype(F32), c_im.astype(F32)
    mi, bm, cm, lr, li = _s5_matrices(mats(s5f), mats(s5b), s5_d, t)
    ut = u.reshape(bsz, nc, t, g, p).transpose(3, 1, 0, 2, 4).reshape(g, nc * bsz, t * p)
    yt = _s5_mix(ut, mi, bm, cm, lr, li, nc, bsz)
    ys = yt.reshape(g, nc, bsz, t, p).transpose(2, 1, 3, 0, 4).reshape(bsz, seq, d_ssm)

    zeros = jnp.zeros((2 * nh,), F32)
    gparams = jnp.stack([
        jnp.concatenate([zeros, dn_a_log_f.astype(F32), dn_a_log_b.astype(F32)]),
        jnp.concatenate([zeros, dn_dt_bias_f.astype(F32), dn_dt_bias_b.astype(F32)])])
    gates_t = gates_t.reshape(bsz, n_gate, seq // DN_CHUNK, DN_CHUNK)
    o = _delta_net(qkv, conv_w.astype(F32), gates, gates_t, gparams, nh)

    return _out_proj(x, mod3, ys, za, o, zb, glu_w, glu_b, dn_norm_w, w_out, out_norm_w, nh)


def kernel(x, c, ada_w, ada_b, norm_w, w_in, conv_w, dn_a_log_f, dn_dt_bias_f, dn_a_log_b,
           dn_dt_bias_b, dn_norm_w, lam_re_f, lam_im_f, log_step_f, b_re_f, b_im_f, c_re_f, c_im_f,
           lam_re_b, lam_im_b, log_step_b, b_re_b, b_im_b, c_re_b, c_im_b, s5_d, glu_w, glu_b,
           w_out, final_norm_w):
    assert ada_w.shape[0] == 1, "single-layer block"
    s5f = (lam_re_f[0], lam_im_f[0], log_step_f[0], b_re_f[0], b_im_f[0], c_re_f[0], c_im_f[0])
    s5b = (lam_re_b[0], lam_im_b[0], log_step_b[0], b_re_b[0], b_im_b[0], c_re_b[0], c_im_b[0])
    return _layer(x, c, ada_w[0], ada_b[0], norm_w[0], w_in[0], conv_w[0],
                  dn_a_log_f[0], dn_dt_bias_f[0], dn_a_log_b[0], dn_dt_bias_b[0], dn_norm_w[0],
                  s5f, s5b, s5_d[0], glu_w[0], glu_b[0], w_out[0], final_norm_w)
```

```python
import functools
import math

import jax
import jax.numpy as jnp
from jax import lax
from jax.experimental import pallas as pl
from jax.experimental.pallas import tpu as pltpu

F32 = jnp.float32
BF16 = jnp.bfloat16
HIGHEST = lax.Precision.HIGHEST

NORM_EPS = 1e-6
SSM_GROUP = 16
SSM_STATE = 64
DN_HEAD_DIM = 128
CONV_K = 5
S5_T = 16
DN_CHUNK = 64
DN_SUB = 16
VMEM_LIMIT = 56 * 1024 * 1024


def _cparams(sem):
    return pltpu.CompilerParams(dimension_semantics=sem, vmem_limit_bytes=VMEM_LIMIT)


def _bdot(a, b):
    return jnp.dot(a.astype(BF16), b.astype(BF16), preferred_element_type=F32)


def _bdot_nt(a, b):
    return lax.dot_general(a.astype(BF16), b.astype(BF16), (((1,), (1,)), ((), ())),
                           preferred_element_type=F32)


def _bdot_tn(a, b):
    return lax.dot_general(a.astype(BF16), b.astype(BF16), (((0,), (0,)), ((), ())),
                           preferred_element_type=F32)


def _split3(x):
    hi = x.astype(BF16)
    r1 = x - hi.astype(F32)
    mid = r1.astype(BF16)
    lo = (r1 - mid.astype(F32)).astype(BF16)
    return hi, mid, lo


def _mod_body(c_ref, w_ref, b_ref, o_ref):
    c = c_ref[...]
    o_ref[...] = _bdot(c * jax.nn.sigmoid(c), w_ref[...]) + b_ref[...]


def _ada_mod(c, ada_w, ada_b):
    bsz, d = c.shape
    n = ada_w.shape[1]
    tn = 1024
    return pl.pallas_call(
        _mod_body,
        grid=(n // tn,),
        in_specs=[pl.BlockSpec((bsz, d), lambda j: (0, 0)),
                  pl.BlockSpec((d, tn), lambda j: (0, j)),
                  pl.BlockSpec((1, tn), lambda j: (0, j))],
        out_specs=pl.BlockSpec((bsz, tn), lambda j: (0, j)),
        out_shape=jax.ShapeDtypeStruct((bsz, n), F32),
        compiler_params=_cparams(("arbitrary",)),
        name="ada_mod",
    )(c, ada_w.astype(BF16), ada_b.reshape(1, n))


def _lane_group(shape):
    return lax.broadcasted_iota(jnp.int32, shape, len(shape) - 1) // SSM_GROUP


def _inproj_body(x_ref, mod_ref, nw_ref, wu_ref, wza_ref, wqkv_ref, wzb_ref, wg_ref, gp_ref,
                 ut_ref, za_ref, qkv_ref, zb_ref, g_ref, slab_ref, *, t_len):
    bsz, tt, d = x_ref.shape
    rows = bsz * tt
    x = x_ref[...]
    ms = jnp.mean(x * x, axis=-1, keepdims=True)
    y = x * lax.rsqrt(ms + NORM_EPS) * nw_ref[...]
    h = (y * (1.0 + mod_ref[:, 1:2, :]) + mod_ref[:, 0:1, :]).reshape(rows, d).astype(BF16)

    def proj(w_ref, out_ref):
        r = jnp.dot(h, w_ref[...], preferred_element_type=F32)
        out_ref[...] = r.reshape(bsz, tt, r.shape[-1])[:, :, :out_ref.shape[-1]].astype(out_ref.dtype)

    proj(wza_ref, za_ref)
    proj(wqkv_ref, qkv_ref)
    proj(wzb_ref, zb_ref)

    r = jnp.dot(h, wg_ref[...], preferred_element_type=F32)
    n_gate = g_ref.shape[-1]
    is_beta = lax.broadcasted_iota(jnp.int32, r.shape, 1) < n_gate // 2
    gv = jnp.where(is_beta, jax.nn.sigmoid(r),
                   -jnp.exp(gp_ref[0:1, :]) * jax.nn.softplus(r + gp_ref[1:2, :]))
    g_ref[...] = gv.reshape(bsz, tt, gv.shape[-1])[:, :, :n_gate]

    u = jnp.dot(h, wu_ref[...], preferred_element_type=F32)
    n_tile = u.shape[-1] // 128
    g_per_tile = 128 // SSM_GROUP
    for j in range(n_tile):
        slab_ref[j] = u[:, j * 128:(j + 1) * 128]
    grp = _lane_group((bsz, 128))
    for cc in range(tt // t_len):
        for j in range(n_tile):
            xs = [slab_ref[j, pl.ds(cc * t_len + s, bsz, stride=tt), :] for s in range(t_len)]
            for g8 in range(g_per_tile):
                for a in range(t_len // g_per_tile):
                    tile = jnp.zeros((bsz, 128), F32)
                    for sp in range(g_per_tile):
                        src = xs[a * g_per_tile + sp]
                        sh = (SSM_GROUP * (sp - g8)) % 128
                        tile = jnp.where(grp == sp, pltpu.roll(src, sh, 1) if sh else src, tile)
                    ut_ref[j * g_per_tile + g8, cc, :, a * 128:(a + 1) * 128] = tile.astype(BF16)


def _in_proj(x, mod3, norm_w, w_in, gparams, d_ssm, d_dn, n_gate, t_len):
    bsz, seq, d = x.shape
    tt = 64
    o1 = d_ssm
    o2 = o1 + d_ssm
    o3 = o2 + 3 * d_dn
    o4 = o3 + d_dn
    wb = w_in.astype(BF16)
    wu, wza, wqkv, wzb = wb[:, :o1], wb[:, o1:o2], wb[:, o2:o3], wb[:, o3:o4]
    wg = jnp.pad(wb[:, o4:], ((0, 0), (0, 128 - n_gate)))
    g = d_ssm // SSM_GROUP
    full = lambda a: pl.BlockSpec(a.shape, lambda i: (0,) * a.ndim)
    row = lambda w: pl.BlockSpec((bsz, tt, w), lambda i: (0, i, 0))
    return pl.pallas_call(
        functools.partial(_inproj_body, t_len=t_len),
        grid=(seq // tt,),
        in_specs=[row(d), full(mod3), pl.BlockSpec((1, d), lambda i: (0, 0)),
                  full(wu), full(wza), full(wqkv), full(wzb), full(wg), full(gparams)],
        out_specs=[pl.BlockSpec((g, tt // t_len, bsz, t_len * SSM_GROUP), lambda i: (0, i, 0, 0)),
                   row(d_ssm), row(3 * d_dn), row(d_dn), row(n_gate)],
        out_shape=[jax.ShapeDtypeStruct((g, seq // t_len, bsz, t_len * SSM_GROUP), BF16),
                   jax.ShapeDtypeStruct((bsz, seq, d_ssm), BF16),
                   jax.ShapeDtypeStruct((bsz, seq, 3 * d_dn), BF16),
                   jax.ShapeDtypeStruct((bsz, seq, d_dn), BF16),
                   jax.ShapeDtypeStruct((bsz, seq, n_gate), F32)],
        scratch_shapes=[pltpu.VMEM((d_ssm // 128, bsz * tt, 128), F32)],
        compiler_params=_cparams(("arbitrary",)),
        name="in_proj",
    )(x, mod3, norm_w.reshape(1, d), wu, wza, wqkv, wzb, wg, gparams)


def _s5_powers(lam_re, lam_im, log_step, b_re, b_im, n_pow):
    dt = jnp.exp(log_step.astype(F32))[:, None]
    lr, li = lam_re.astype(F32), lam_im.astype(F32)
    mag = jnp.exp(lr * dt)
    ab_re, ab_im = mag * jnp.cos(li * dt), mag * jnp.sin(li * dt)
    nr, ni = ab_re - 1.0, ab_im
    den = lr * lr + li * li
    f_re = (nr * lr + ni * li) / den
    f_im = (ni * lr - nr * li) / den
    br, bi = b_re.astype(F32), b_im.astype(F32)
    bb_re = f_re[..., None] * br - f_im[..., None] * bi
    bb_im = f_re[..., None] * bi + f_im[..., None] * br
    tau = jnp.arange(n_pow, dtype=F32)[:, None, None]
    pmag = jnp.exp(tau * (lr * dt)[None])
    ang = tau * (li * dt)[None]
    return pmag * jnp.cos(ang), pmag * jnp.sin(ang), bb_re, bb_im


def _s5_matrices(pf, pb, s5_d, t_len):
    (pfr, pfi, bfr, bfi, cfr, cfi) = pf
    (pbr, pbi, bbr, bbi, cbr, cbi) = pb
    g, n, p = bfr.shape
    t = t_len
    ein = functools.partial(jnp.einsum, precision=HIGHEST)

    def lag_kernels(pr, pi, br, bi, cr, ci):
        xr = pr[:t, :, :, None] * br[None] - pi[:t, :, :, None] * bi[None]
        xi = pr[:t, :, :, None] * bi[None] + pi[:t, :, :, None] * br[None]
        return ein("gpn,tgnq->tgpq", cr, xr) - ein("gpn,tgnq->tgpq", ci, xi)

    kf = lag_kernels(pfr, pfi, bfr, bfi, cfr, cfi)
    kb = lag_kernels(pbr, pbi, bbr, bbi, cbr, cbi)
    dmat = jnp.eye(p, dtype=F32)[None] * s5_d.astype(F32).reshape(g, p)[:, :, None]
    k0 = kf[0] + kb[0] + dmat
    kall = jnp.concatenate([kb[1:][::-1], k0[None], kf[1:]], axis=0)
    idx = jnp.arange(t)[None, :] - jnp.arange(t)[:, None] + (t - 1)
    mi = kall[idx]
    mi = mi.transpose(2, 0, 4, 1, 3).reshape(g, t * p, t * p)

    def in_map(pr, pi, br, bi, order):
        prs, pis = pr[order], pi[order]
        re = prs[..., None] * br[None] - pis[..., None] * bi[None]
        im = prs[..., None] * bi[None] + pis[..., None] * br[None]
        f = lambda a: a.transpose(1, 0, 3, 2).reshape(g, t * p, n)
        return f(re), f(im)

    s_idx = jnp.arange(t)
    bfre, bfim = in_map(pfr, pfi, bfr, bfi, t - 1 - s_idx)
    bbre, bbim = in_map(pbr, pbi, bbr, bbi, s_idx)
    bm = jnp.concatenate([bfre, bbre, bfim, bbim], axis=-1)

    def out_map(pr, pi, cr, ci, order):
        prs, pis = pr[order], pi[order]
        wre = cr[None] * prs[:, :, None, :] - ci[None] * pis[:, :, None, :]
        wim = -(cr[None] * pis[:, :, None, :] + ci[None] * prs[:, :, None, :])
        f = lambda a: a.transpose(1, 3, 0, 2).reshape(g, n, t * p)
        return f(wre), f(wim)

    cfre, cfim = out_map(pfr, pfi, cfr, cfi, s_idx + 1)
    cbre, cbim = out_map(pbr, pbi, cbr, cbi, t - s_idx)
    cm = jnp.concatenate([cfre, cbre, cfim, cbim], axis=1)
    lr = jnp.concatenate([pfr[t], pbr[t]], axis=-1)[:, None, :]
    li = jnp.concatenate([pfi[t], pbi[t]], axis=-1)[:, None, :]
    return mi.astype(BF16), bm.astype(BF16), cm.astype(BF16), lr, li


def _s5_body(u_ref, mi_ref, bm_ref, cm_ref, lr_ref, li_ref, y_ref, con_ref, s0_ref, s1_ref,
             *, gb, nc, bsz):
    n2 = lr_ref.shape[-1]
    for g in range(gb):
        con_ref[g] = jnp.dot(u_ref[g], bm_ref[g], preferred_element_type=F32)
    lane = lax.broadcasted_iota(jnp.int32, (bsz, 2 * n2), 1)
    is_fwd = (lane % n2) < (n2 // 2)
    lrs = [jnp.broadcast_to(lr_ref[g], (bsz, n2)) for g in range(gb)]
    lis = [jnp.broadcast_to(li_ref[g], (bsz, n2)) for g in range(gb)]

    def step(i, carry):
        rf = pl.multiple_of(i * bsz, bsz)
        rb = pl.multiple_of((nc - 1 - i) * bsz, bsz)
        out = []
        for g in range(gb):
            sre, sim = carry[2 * g], carry[2 * g + 1]
            st = jnp.concatenate([sre, sim], axis=-1).astype(BF16)
            s0_ref[g, pl.ds(rf, bsz), :] = st
            s1_ref[g, pl.ds(rb, bsz), :] = st
            c = jnp.where(is_fwd, con_ref[g, pl.ds(rf, bsz), :], con_ref[g, pl.ds(rb, bsz), :])
            nre = lrs[g] * sre - lis[g] * sim + c[:, :n2]
            nim = lrs[g] * sim + lis[g] * sre + c[:, n2:]
            out += [nre, nim]
        return tuple(out)

    zero = jnp.zeros((bsz, n2), F32)
    lax.fori_loop(0, nc, step, (zero,) * (2 * gb))

    rows = nc * bsz
    lane_all = lax.broadcasted_iota(jnp.int32, (rows, 2 * n2), 1)
    sel = (lane_all % n2) < (n2 // 2)
    for g in range(gb):
        xs = jnp.where(sel, s0_ref[g], s1_ref[g])
        y = (jnp.dot(u_ref[g], mi_ref[g], preferred_element_type=F32)
             + jnp.dot(xs, cm_ref[g], preferred_element_type=F32))
        y_ref[g] = jax.nn.gelu(y).astype(BF16)


def _s5_mix(ut, mi, bm, cm, lr, li, nc, bsz):
    g, rows, tp = ut.shape
    gb = 2
    n4 = bm.shape[-1]
    blk = lambda a: pl.BlockSpec((gb,) + a.shape[1:], lambda j: (j,) + (0,) * (a.ndim - 1))
    return pl.pallas_call(
        functools.partial(_s5_body, gb=gb, nc=nc, bsz=bsz),
        grid=(g // gb,),
        in_specs=[blk(ut), blk(mi), blk(bm), blk(cm), blk(lr), blk(li)],
        out_specs=blk(ut),
        out_shape=jax.ShapeDtypeStruct(ut.shape, BF16),
        scratch_shapes=[pltpu.VMEM((gb, rows, n4), F32),
                        pltpu.VMEM((gb, rows, n4), BF16),
                        pltpu.VMEM((gb, rows, n4), BF16)],
        compiler_params=_cparams(("arbitrary",)),
        name="s5_mix",
    )(ut, mi, bm, cm, lr, li)


def _dn_body(q_ref, k_ref, v_ref, cq_ref, ck_ref, cv_ref, gates_ref, o_ref,
             pad_ref, qs_ref, ks_ref, vs_ref, bcol_ref, gcol_ref, grow_ref,
             mq_ref, n_ref, op_ref, el_ref, *, seq, ch, nh):
    hd = q_ref.shape[-1]
    h = pl.program_id(1)
    nch = seq // ch
    halo = 8
    rb = 512
    nrb = seq // rb
    ng = gates_ref.shape[-1]

    pad_ref[pl.ds(0, halo), :] = jnp.zeros((halo, hd), F32)
    pad_ref[pl.ds(halo + seq, halo), :] = jnp.zeros((halo, hd), F32)

    def conv_into(src_ref, w_ref, dst_ref, mode):
        def fill(i, _):
            r0 = pl.multiple_of(i * rb, rb)
            pad_ref[pl.ds(halo + r0, rb), :] = src_ref[0, pl.ds(r0, rb), :].astype(F32)
            return 0
        lax.fori_loop(0, nrb, fill, 0)

        def blk(i, _):
            r0 = pl.multiple_of(i * rb, rb)
            acc = jnp.zeros((rb, hd), F32)
            for j in range(CONV_K):
                acc = acc + pad_ref[pl.ds(halo + r0 + (j - CONV_K // 2), rb), :] * w_ref[j:j + 1, :]
            a = acc * jax.nn.sigmoid(acc)
            if mode == "q":
                a = a * lax.rsqrt(jnp.sum(a * a, axis=-1, keepdims=True) + NORM_EPS) * (hd ** -0.5)
            elif mode == "k":
                a = a * lax.rsqrt(jnp.sum(a * a, axis=-1, keepdims=True) + NORM_EPS)
            dst_ref[pl.ds(r0, rb), :] = a.astype(dst_ref.dtype)
            return 0
        lax.fori_loop(0, nrb, blk, 0)

    conv_into(q_ref, cq_ref, qs_ref, "q")
    conv_into(k_ref, ck_ref, ks_ref, "k")
    conv_into(v_ref, cv_ref, vs_ref, "v")

    cols = (h, nh + h, 2 * nh + h, 3 * nh + h)
    gb = 256
    bi = lax.broadcasted_iota(jnp.int32, (gb, gb), 0)
    bj = lax.broadcasted_iota(jnp.int32, (gb, gb), 1)
    same_chunk = (bi // ch) == (bj // ch)
    pre = (same_chunk & (bi >= bj)).astype(BF16)
    suf = (same_chunk & (bi <= bj)).astype(BF16)
    sel_c = lax.broadcasted_iota(jnp.int32, (gb, ng), 1)

    def col_of(x, col):
        return jnp.broadcast_to(jnp.sum(jnp.where(sel_c == col, x, 0.0), axis=-1, keepdims=True),
                                (gb, hd))

    def gate_blk(i, _):
        r0 = pl.multiple_of(i * gb, gb)
        gt = gates_ref[0, pl.ds(r0, gb), :]
        parts = _split3(gt)
        for d in range(2):
            tri = pre if d == 0 else suf
            cs = sum(jnp.dot(tri, p, preferred_element_type=F32) for p in parts)
            gcol_ref[d, pl.ds(r0, gb), :] = col_of(cs, cols[2 + d])
            bcol_ref[d, pl.ds(r0, gb), :] = col_of(gt, cols[d])
        return 0
    lax.fori_loop(0, seq // gb, gate_blk, 0)

    ii = lax.broadcasted_iota(jnp.int32, (ch, ch), 0)
    jj = lax.broadcasted_iota(jnp.int32, (ch, ch), 1)
    pos_r = lax.broadcasted_iota(jnp.int32, (seq, hd), 0) % ch
    pos_l = lax.broadcasted_iota(jnp.int32, (seq, hd), 1)
    seg = (lax.broadcasted_iota(jnp.int32, (nch, seq), 1) // ch
           == lax.broadcasted_iota(jnp.int32, (nch, seq), 0)).astype(BF16)
    for d in range(2):
        parts = _split3(jnp.where(pos_r == pos_l, gcol_ref[d], 0.0))
        grow_ref[d] = sum(jnp.dot(seg, p, preferred_element_type=F32) for p in parts)[:, :ch]

    masks = (ii >= jj, ii <= jj)
    stricts = (ii > jj, ii < jj)
    last_rows = (ch - 1, 0)
    eye = (ii == jj).astype(F32)
    same_blk = (ii // DN_SUB) == (jj // DN_SUB)
    n_sub_lvl = int(math.log2(DN_SUB))
    n_blk_lvl = int(math.log2(ch // DN_SUB))

    def prep_load(c, d):
        r0 = pl.multiple_of(c * ch, ch)
        return (d, qs_ref[pl.ds(r0, ch), :], ks_ref[pl.ds(r0, ch), :], vs_ref[pl.ds(r0, ch), :],
                bcol_ref[d, pl.ds(r0, ch), :], gcol_ref[d, pl.ds(r0, ch), :],
                grow_ref[d, pl.ds(c, 1), :])

    def prep_compute(loaded, tick):
        chains = [dict(d=v[0], kk=_bdot_nt(v[2], v[2])) for v in loaded]
        for cn, v in zip(chains, loaded):
            cn["qk"] = _bdot_nt(v[1], v[2])
        tick()
        for cn, v in zip(chains, loaded):
            d, qc, kc, vc, beta, gc, gr = v
            kf, qf, vc = kc.astype(F32), qc.astype(F32), vc.astype(F32)
            mask = masks[d]
            decay = jnp.where(mask, jnp.exp(jnp.where(mask, gc[:, :ch] - gr, 0.0)), 0.0)
            a = jnp.where(stricts[d], beta[:, :ch] * cn["kk"] * decay, 0.0)
            a_d = jnp.where(same_blk, a, 0.0)
            egc = jnp.exp(gc)
            g_last = gc[last_rows[d]:last_rows[d] + 1, :]
            cn.update(a_d=a_d, dinv=eye - a_d,
                      rhs=jnp.concatenate([kf * (beta * egc), vc * beta, a - a_d], axis=-1),
                      attn=jnp.where(mask, cn["qk"] * decay, 0.0), qd=qf * egc,
                      kt=kf * jnp.exp(g_last - gc), el=jnp.exp(g_last))
        for cn in chains:
            cn["p"] = _bdot(cn["a_d"], cn["a_d"])
        tick()
        for lvl in range(1, n_sub_lvl):
            for cn in chains:
                if lvl + 1 < n_sub_lvl:
                    both = _bdot(jnp.concatenate([cn["dinv"], cn["p"]], axis=0), cn["p"])
                    cn["dinv"] = cn["dinv"] + both[:ch]
                    cn["p"] = both[ch:]
                else:
                    cn["dinv"] = cn["dinv"] + _bdot(cn["dinv"], cn["p"])
            tick()
        for cn in chains:
            first = _bdot(cn["dinv"], cn["rhs"])
            cn["sol"] = first[:, :2 * hd]
            cn["pw"] = first[:, 2 * hd:]
        tick()
        for lvl in range(n_blk_lvl):
            for cn in chains:
                upd = _bdot(cn["pw"], cn["sol"])
                cn["sol"] = cn["sol"] - upd if lvl == 0 else cn["sol"] + upd
                if lvl + 1 < n_blk_lvl:
                    cn["pw"] = _bdot(cn["pw"], cn["pw"])
            tick()
        for cn in chains:
            cn["x1"] = _bdot_tn(cn["kt"], cn["sol"])
        tick()
        for cn in chains:
            cn["x2"] = _bdot(cn["attn"], cn["sol"])
        tick()
        outs = []
        for cn in chains:
            x1, x2 = cn["x1"], cn["x2"]
            outs.append((jnp.concatenate([-x1[:, :hd], cn["qd"] - x2[:, :hd]], axis=0).astype(BF16),
                         x1[:, hd:], x2[:, hd:], cn["el"]))
        return outs

    def prep_store(c, d, out):
        r0 = pl.multiple_of(c * ch, ch)
        mq_ref[d, c] = out[0]
        n_ref[d, c] = out[1]
        op_ref[d, pl.ds(r0, ch), :] = out[2]
        el_ref[d, pl.ds(c, 1), :] = out[3]

    na_par = 8
    n_iter = nch // na_par

    def chunk_of(i, j, d):
        return i * na_par + j if d == 0 else nch - 1 - (i * na_par + j)

    def prep_set(i, tick):
        todo = [(chunk_of(i, j, d), d) for j in range(na_par) for d in range(2)]
        outs = prep_compute([prep_load(c, d) for c, d in todo], tick)
        for (c, d), out in zip(todo, outs):
            prep_store(c, d, out)

    o_ref[...] = jnp.zeros(o_ref.shape, F32)

    def state_step(i, j, states):
        new = []
        for d in range(2):
            c = chunk_of(i, j, d)
            r0 = pl.multiple_of(c * ch, ch)
            r = jnp.dot(mq_ref[d, c], states[d].astype(BF16), preferred_element_type=F32)
            o_ref[0, pl.ds(r0, ch), :] += r[hd:] + op_ref[d, pl.ds(r0, ch), :]
            new.append(states[d] * el_ref[d, pl.ds(c, 1), :] + r[:hd] + n_ref[d, c])
        return new

    prep_set(0, lambda: None)

    def pipelined(i, carry):
        box = dict(states=list(carry), j=0)

        def tick():
            if box["j"] < na_par:
                box["states"] = state_step(i - 1, box["j"], box["states"])
                box["j"] += 1
        prep_set(i, tick)
        while box["j"] < na_par:
            tick()
        return tuple(box["states"])

    z = jnp.zeros((hd, hd), F32)
    states = lax.fori_loop(1, n_iter, pipelined, (z, z))
    states = list(states)
    for j in range(na_par):
        states = state_step(n_iter - 1, j, states)


def _delta_net(qkv, conv_w, gates, nh):
    bsz, seq, _ = qkv.shape
    hd = DN_HEAD_DIM
    ch = DN_CHUNK
    nch = seq // ch
    ng = gates.shape[-1]
    col = lambda off: pl.BlockSpec((1, seq, hd), lambda b, h: (b, 0, off + h))
    cw = lambda off: pl.BlockSpec((CONV_K, hd), lambda b, h: (0, off + h))
    return pl.pallas_call(
        functools.partial(_dn_body, seq=seq, ch=ch, nh=nh),
        grid=(bsz, nh),
        in_specs=[col(0), col(nh), col(2 * nh), cw(0), cw(nh), cw(2 * nh),
                  pl.BlockSpec((1, seq, ng), lambda b, h: (b, 0, 0))],
        out_specs=pl.BlockSpec((1, seq, hd), lambda b, h: (b, 0, h)),
        out_shape=jax.ShapeDtypeStruct((bsz, seq, nh * hd), F32),
        scratch_shapes=[pltpu.VMEM((seq + 16, hd), F32),
                        pltpu.VMEM((seq, hd), BF16),
                        pltpu.VMEM((seq, hd), BF16),
                        pltpu.VMEM((seq, hd), BF16),
                        pltpu.VMEM((2, seq, hd), F32),
                        pltpu.VMEM((2, seq, hd), F32),
                        pltpu.VMEM((2, nch, ch), F32),
                        pltpu.VMEM((2, nch, hd + ch, hd), BF16),
                        pltpu.VMEM((2, nch, hd, hd), F32),
                        pltpu.VMEM((2, seq, hd), F32),
                        pltpu.VMEM((2, nch, hd), F32)],
        compiler_params=_cparams(("arbitrary", "arbitrary")),
        name="delta_net",
    )(qkv, qkv, qkv, conv_w, conv_w, conv_w, gates)


def _out_body(x_ref, mod_ref, yt_ref, za_ref, o_ref, zb_ref, gw_ref, gbias_ref, dnw_ref,
              wo_ref, fw_ref, out_ref, slab_ref, *, nh, t_len):
    bsz, tt, d = x_ref.shape
    rows = bsz * tt
    d_ssm = za_ref.shape[-1]
    hd = o_ref.shape[-1] // nh
    n_tile = d_ssm // 128
    g_per_tile = 128 // SSM_GROUP

    grp = _lane_group((bsz, 128))
    for cc in range(tt // t_len):
        for j in range(n_tile):
            for a in range(t_len // g_per_tile):
                ys = [yt_ref[j * g_per_tile + g8, cc, :, a * 128:(a + 1) * 128].astype(F32)
                      for g8 in range(g_per_tile)]
                for tp in range(g_per_tile):
                    tile = jnp.zeros((bsz, 128), F32)
                    for g8 in range(g_per_tile):
                        sh = (SSM_GROUP * (g8 - tp)) % 128
                        tile = jnp.where(grp == g8, pltpu.roll(ys[g8], sh, 1) if sh else ys[g8], tile)
                    t = a * g_per_tile + tp
                    slab_ref[j, pl.ds(cc * t_len + t, bsz, stride=tt), :] = tile
    ysn = jnp.concatenate([slab_ref[j] for j in range(n_tile)], axis=-1).astype(BF16)

    glu = jnp.dot(ysn, gw_ref[...], preferred_element_type=F32) + gbias_ref[...]
    za = za_ref[...].reshape(rows, d_ssm).astype(F32)
    ya = glu[:, :d_ssm] * jax.nn.sigmoid(glu[:, d_ssm:]) * (za * jax.nn.sigmoid(za))
    mix = jnp.dot(ya.astype(BF16), wo_ref[pl.ds(0, d_ssm), :], preferred_element_type=F32)
    for hh in range(nh):
        o = o_ref[:, :, hh * hd:(hh + 1) * hd].reshape(rows, hd)
        zb = zb_ref[:, :, hh * hd:(hh + 1) * hd].reshape(rows, hd).astype(F32)
        o = o * lax.rsqrt(jnp.mean(o * o, axis=-1, keepdims=True) + NORM_EPS) * dnw_ref[...]
        yb = (o * (zb * jax.nn.sigmoid(zb))).astype(BF16)
        mix = mix + jnp.dot(yb, wo_ref[pl.ds(d_ssm + hh * hd, hd), :], preferred_element_type=F32)
    r = x_ref[...] + mod_ref[:, 2:3, :] * mix.reshape(bsz, tt, d)
    out_ref[...] = r * lax.rsqrt(jnp.mean(r * r, axis=-1, keepdims=True) + NORM_EPS) * fw_ref[...]


def _out_proj(x, mod3, yt, za, o, zb, glu_w, glu_b, dn_norm_w, w_out, final_norm_w, nh, t_len):
    bsz, seq, d = x.shape
    tt = 64
    g = yt.shape[0]
    full = lambda a: pl.BlockSpec(a.shape, lambda i: (0,) * a.ndim)
    row = lambda a: pl.BlockSpec((bsz, tt, a.shape[-1]), lambda i: (0, i, 0))
    gw = glu_w.astype(BF16)
    gbias = glu_b.reshape(1, -1)
    dnw = dn_norm_w.reshape(1, -1)
    wo = w_out.astype(BF16)
    fw = final_norm_w.reshape(1, -1)
    return pl.pallas_call(
        functools.partial(_out_body, nh=nh, t_len=t_len),
        grid=(seq // tt,),
        in_specs=[row(x), full(mod3),
                  pl.BlockSpec((g, tt // t_len, bsz, yt.shape[-1]), lambda i: (0, i, 0, 0)),
                  row(za), row(o), row(zb),
                  full(gw), full(gbias), full(dnw), full(wo), full(fw)],
        out_specs=row(x),
        out_shape=jax.ShapeDtypeStruct(x.shape, F32),
        scratch_shapes=[pltpu.VMEM((za.shape[-1] // 128, bsz * tt, 128), F32)],
        compiler_params=_cparams(("arbitrary",)),
        name="out_proj",
    )(x, mod3, yt, za, o, zb, gw, gbias, dnw, wo, fw)


def _layer(x, c, ada_w, ada_b, norm_w, w_in, conv_w,
           dn_a_log_f, dn_dt_bias_f, dn_a_log_b, dn_dt_bias_b, dn_norm_w,
           s5f, s5b, s5_d, glu_w, glu_b, w_out, out_norm_w):
    bsz, seq, d = x.shape
    d_ssm = s5_d.shape[0]
    d_dn = w_out.shape[0] - d_ssm
    nh = d_dn // DN_HEAD_DIM
    n_gate = 4 * nh
    p = SSM_GROUP
    g = d_ssm // p
    t = S5_T
    nc = seq // t

    mod3 = _ada_mod(c, ada_w, ada_b).reshape(bsz, 3, d)
    zeros = jnp.zeros((2 * nh,), F32)
    gparams = jnp.stack([
        jnp.concatenate([zeros, dn_a_log_f.astype(F32), dn_a_log_b.astype(F32)]),
        jnp.concatenate([zeros, dn_dt_bias_f.astype(F32), dn_dt_bias_b.astype(F32)])])
    gparams = jnp.pad(gparams, ((0, 0), (0, 128 - n_gate)))
    ut, za, qkv, zb, gates = _in_proj(x, mod3, norm_w, w_in, gparams, d_ssm, d_dn, n_gate, t)

    def mats(prm):
        lam_re, lam_im, log_step, b_re, b_im, c_re, c_im = prm
        pr, pi, br, bi = _s5_powers(lam_re, lam_im, log_step, b_re, b_im, t + 1)
        return pr, pi, br, bi, c_re.astype(F32), c_im.astype(F32)
    mi, bm, cm, lr, li = _s5_matrices(mats(s5f), mats(s5b), s5_d, t)
    yt = _s5_mix(ut.reshape(g, nc * bsz, t * p), mi, bm, cm, lr, li, nc, bsz)
    yt = yt.reshape(g, nc, bsz, t * p)

    o = _delta_net(qkv, conv_w.astype(F32), gates, nh)

    return _out_proj(x, mod3, yt, za, o, zb, glu_w, glu_b, dn_norm_w, w_out, out_norm_w, nh, t)


def kernel(x, c, ada_w, ada_b, norm_w, w_in, conv_w, dn_a_log_f, dn_dt_bias_f, dn_a_log_b,
           dn_dt_bias_b, dn_norm_w, lam_re_f, lam_im_f, log_step_f, b_re_f, b_im_f, c_re_f, c_im_f,
           lam_re_b, lam_im_b, log_step_b, b_re_b, b_im_b, c_re_b, c_im_b, s5_d, glu_w, glu_b,
           w_out, final_norm_w):
    assert ada_w.shape[0] == 1, "single-layer block"
    s5f = (lam_re_f[0], lam_im_f[0], log_step_f[0], b_re_f[0], b_im_f[0], c_re_f[0], c_im_f[0])
    s5b = (lam_re_b[0], lam_im_b[0], log_step_b[0], b_re_b[0], b_im_b[0], c_re_b[0], c_im_b[0])
    return _layer(x, c, ada_w[0], ada_b[0], norm_w[0], w_in[0], conv_w[0],
                  dn_a_log_f[0], dn_dt_bias_f[0], dn_a_log_b[0], dn_dt_bias_b[0], dn_norm_w[0],
                  s5f, s5b, s5_d[0], glu_w[0], glu_b[0], w_out[0], final_norm_w)
```

```python
import functools
import math

import jax
import jax.numpy as jnp
from jax import lax
from jax.experimental import pallas as pl
from jax.experimental.pallas import tpu as pltpu

F32 = jnp.float32
BF16 = jnp.bfloat16
HIGHEST = lax.Precision.HIGHEST

NORM_EPS = 1e-6
SSM_GROUP = 16
SSM_STATE = 64
DN_HEAD_DIM = 128
CONV_K = 5
S5_T = 16
DN_CHUNK = 64
DN_SUB = 16
VMEM_LIMIT = 56 * 1024 * 1024


def _cparams(sem):
    return pltpu.CompilerParams(dimension_semantics=sem, vmem_limit_bytes=VMEM_LIMIT)


def _bdot(a, b):
    return jnp.dot(a.astype(BF16), b.astype(BF16), preferred_element_type=F32)


def _bdot_nt(a, b):
    return lax.dot_general(a.astype(BF16), b.astype(BF16), (((1,), (1,)), ((), ())),
                           preferred_element_type=F32)


def _mod_body(c_ref, w_ref, b_ref, o_ref):
    c = c_ref[...]
    o_ref[...] = _bdot(c * jax.nn.sigmoid(c), w_ref[...]) + b_ref[...]


def _ada_mod(c, ada_w, ada_b):
    bsz, d = c.shape
    n = ada_w.shape[1]
    tn = 1024
    return pl.pallas_call(
        _mod_body,
        grid=(n // tn,),
        in_specs=[pl.BlockSpec((bsz, d), lambda j: (0, 0)),
                  pl.BlockSpec((d, tn), lambda j: (0, j)),
                  pl.BlockSpec((1, tn), lambda j: (0, j))],
        out_specs=pl.BlockSpec((bsz, tn), lambda j: (0, j)),
        out_shape=jax.ShapeDtypeStruct((bsz, n), F32),
        compiler_params=_cparams(("arbitrary",)),
        name="ada_mod",
    )(c, ada_w.astype(BF16), ada_b.reshape(1, n))


def _lane_group(shape):
    return lax.broadcasted_iota(jnp.int32, shape, len(shape) - 1) // SSM_GROUP


def _inproj_body(x_ref, mod_ref, nw_ref, wu_ref, wza_ref, wqkv_ref, wzb_ref, wg_ref, gp_ref,
                 ut_ref, za_ref, qkv_ref, zb_ref, g_ref, slab_ref, *, t_len, dn_chunk):
    bsz, tt, d = x_ref.shape
    rows = bsz * tt
    x = x_ref[...]
    ms = jnp.mean(x * x, axis=-1, keepdims=True)
    y = x * lax.rsqrt(ms + NORM_EPS) * nw_ref[...]
    h = (y * (1.0 + mod_ref[:, 1:2, :]) + mod_ref[:, 0:1, :]).reshape(rows, d).astype(BF16)

    u = jnp.dot(h, wu_ref[...], preferred_element_type=F32)
    n_tile = u.shape[-1] // 128
    g_per_tile = 128 // SSM_GROUP
    for j in range(n_tile):
        slab_ref[j] = u[:, j * 128:(j + 1) * 128]
    grp = _lane_group((bsz, 128))
    for cc in range(tt // t_len):
        for j in range(n_tile):
            xs = [slab_ref[j, pl.ds(cc * t_len + s, bsz, stride=tt), :] for s in range(t_len)]
            for g8 in range(g_per_tile):
                for a in range(t_len // g_per_tile):
                    tile = jnp.zeros((bsz, 128), F32)
                    for sp in range(g_per_tile):
                        src = xs[a * g_per_tile + sp]
                        sh = (SSM_GROUP * (sp - g8)) % 128
                        tile = jnp.where(grp == sp, pltpu.roll(src, sh, 1) if sh else src, tile)
                    ut_ref[j * g_per_tile + g8, cc, :, a * 128:(a + 1) * 128] = tile.astype(BF16)

    r = jnp.dot(h, wg_ref[...], preferred_element_type=F32)
    n_gate = g_ref.shape[-1]
    lane = lax.broadcasted_iota(jnp.int32, r.shape, 1)
    gl = -jnp.exp(gp_ref[0:1, :]) * jax.nn.softplus(r + gp_ref[1:2, :])
    pos = lax.broadcasted_iota(jnp.int32, r.shape, 0) % dn_chunk
    pre = gl
    suf = gl
    sh = 1
    while sh < dn_chunk:
        pre = pre + jnp.where(pos >= sh, pltpu.roll(pre, sh, 0), 0.0)
        suf = suf + jnp.where(pos < dn_chunk - sh, pltpu.roll(suf, rows - sh, 0), 0.0)
        sh *= 2
    gv = jnp.where(lane < n_gate // 2, jax.nn.sigmoid(r), jnp.where(lane < 3 * n_gate // 4, pre, suf))
    g_ref[...] = gv.reshape(bsz, tt, gv.shape[-1])[:, :, :n_gate]

    def proj(w_ref, out_ref):
        r = jnp.dot(h, w_ref[...], preferred_element_type=F32)
        out_ref[...] = r.reshape(bsz, tt, r.shape[-1])[:, :, :out_ref.shape[-1]].astype(out_ref.dtype)

    proj(wza_ref, za_ref)
    proj(wqkv_ref, qkv_ref)
    proj(wzb_ref, zb_ref)


def _in_proj(x, mod3, norm_w, w_in, gparams, d_ssm, d_dn, n_gate, t_len):
    bsz, seq, d = x.shape
    tt = DN_CHUNK
    o1 = d_ssm
    o2 = o1 + d_ssm
    o3 = o2 + 3 * d_dn
    o4 = o3 + d_dn
    wb = w_in.astype(BF16)
    wu, wza, wqkv, wzb = wb[:, :o1], wb[:, o1:o2], wb[:, o2:o3], wb[:, o3:o4]
    wg = jnp.pad(wb[:, o4:], ((0, 0), (0, 128 - n_gate)))
    g = d_ssm // SSM_GROUP
    full = lambda a: pl.BlockSpec(a.shape, lambda i: (0,) * a.ndim)
    row = lambda w: pl.BlockSpec((bsz, tt, w), lambda i: (0, i, 0))
    return pl.pallas_call(
        functools.partial(_inproj_body, t_len=t_len, dn_chunk=DN_CHUNK),
        grid=(seq // tt,),
        in_specs=[row(d), full(mod3), pl.BlockSpec((1, d), lambda i: (0, 0)),
                  full(wu), full(wza), full(wqkv), full(wzb), full(wg), full(gparams)],
        out_specs=[pl.BlockSpec((g, tt // t_len, bsz, t_len * SSM_GROUP), lambda i: (0, i, 0, 0)),
                   row(d_ssm), row(3 * d_dn), row(d_dn), row(n_gate)],
        out_shape=[jax.ShapeDtypeStruct((g, seq // t_len, bsz, t_len * SSM_GROUP), BF16),
                   jax.ShapeDtypeStruct((bsz, seq, d_ssm), BF16),
                   jax.ShapeDtypeStruct((bsz, seq, 3 * d_dn), BF16),
                   jax.ShapeDtypeStruct((bsz, seq, d_dn), BF16),
                   jax.ShapeDtypeStruct((bsz, seq, n_gate), F32)],
        scratch_shapes=[pltpu.VMEM((d_ssm // 128, bsz * tt, 128), F32)],
        compiler_params=_cparams(("arbitrary",)),
        name="in_proj",
    )(x, mod3, norm_w.reshape(1, d), wu, wza, wqkv, wzb, wg, gparams)


def _s5_powers(lam_re, lam_im, log_step, b_re, b_im, n_pow):
    dt = jnp.exp(log_step.astype(F32))[:, None]
    lr, li = lam_re.astype(F32), lam_im.astype(F32)
    mag = jnp.exp(lr * dt)
    ab_re, ab_im = mag * jnp.cos(li * dt), mag * jnp.sin(li * dt)
    nr, ni = ab_re - 1.0, ab_im
    den = lr * lr + li * li
    f_re = (nr * lr + ni * li) / den
    f_im = (ni * lr - nr * li) / den
    br, bi = b_re.astype(F32), b_im.astype(F32)
    bb_re = f_re[..., None] * br - f_im[..., None] * bi
    bb_im = f_re[..., None] * bi + f_im[..., None] * br
    tau = jnp.arange(n_pow, dtype=F32)[:, None, None]
    pmag = jnp.exp(tau * (lr * dt)[None])
    ang = tau * (li * dt)[None]
    return pmag * jnp.cos(ang), pmag * jnp.sin(ang), bb_re, bb_im


def _s5_matrices(pf, pb, s5_d, t_len):
    (pfr, pfi, bfr, bfi, cfr, cfi) = pf
    (pbr, pbi, bbr, bbi, cbr, cbi) = pb
    g, n, p = bfr.shape
    t = t_len
    ein = functools.partial(jnp.einsum, precision=HIGHEST)

    def lag_kernels(pr, pi, br, bi, cr, ci):
        xr = pr[:t, :, :, None] * br[None] - pi[:t, :, :, None] * bi[None]
        xi = pr[:t, :, :, None] * bi[None] + pi[:t, :, :, None] * br[None]
        return ein("gpn,tgnq->tgpq", cr, xr) - ein("gpn,tgnq->tgpq", ci, xi)

    kf = lag_kernels(pfr, pfi, bfr, bfi, cfr, cfi)
    kb = lag_kernels(pbr, pbi, bbr, bbi, cbr, cbi)
    dmat = jnp.eye(p, dtype=F32)[None] * s5_d.astype(F32).reshape(g, p)[:, :, None]
    k0 = kf[0] + kb[0] + dmat
    kall = jnp.concatenate([kb[1:][::-1], k0[None], kf[1:]], axis=0)
    idx = jnp.arange(t)[None, :] - jnp.arange(t)[:, None] + (t - 1)
    mi = kall[idx]
    mi = mi.transpose(2, 0, 4, 1, 3).reshape(g, t * p, t * p)

    def in_map(pr, pi, br, bi, order):
        prs, pis = pr[order], pi[order]
        re = prs[..., None] * br[None] - pis[..., None] * bi[None]
        im = prs[..., None] * bi[None] + pis[..., None] * br[None]
        f = lambda a: a.transpose(1, 0, 3, 2).reshape(g, t * p, n)
        return f(re), f(im)

    s_idx = jnp.arange(t)
    bfre, bfim = in_map(pfr, pfi, bfr, bfi, t - 1 - s_idx)
    bbre, bbim = in_map(pbr, pbi, bbr, bbi, s_idx)
    bm = jnp.concatenate([bfre, bbre, bfim, bbim], axis=-1)

    def out_map(pr, pi, cr, ci, order):
        prs, pis = pr[order], pi[order]
        wre = cr[None] * prs[:, :, None, :] - ci[None] * pis[:, :, None, :]
        wim = -(cr[None] * pis[:, :, None, :] + ci[None] * prs[:, :, None, :])
        f = lambda a: a.transpose(1, 3, 0, 2).reshape(g, n, t * p)
        return f(wre), f(wim)

    cfre, cfim = out_map(pfr, pfi, cfr, cfi, s_idx + 1)
    cbre, cbim = out_map(pbr, pbi, cbr, cbi, t - s_idx)
    cm = jnp.concatenate([cfre, cbre, cfim, cbim], axis=1)
    lr = jnp.concatenate([pfr[t], pbr[t]], axis=-1)[:, None, :]
    li = jnp.concatenate([pfi[t], pbi[t]], axis=-1)[:, None, :]
    return mi.astype(BF16), bm.astype(BF16), cm.astype(BF16), lr, li


def _s5_body(u_ref, mi_ref, bm_ref, cm_ref, lr_ref, li_ref, y_ref, con_ref, s0_ref, s1_ref,
             *, gb, nc, bsz):
    n2 = lr_ref.shape[-1]
    for g in range(gb):
        con_ref[g] = jnp.dot(u_ref[g], bm_ref[g], preferred_element_type=F32)
    lane = lax.broadcasted_iota(jnp.int32, (bsz, 2 * n2), 1)
    is_fwd = (lane % n2) < (n2 // 2)
    lrs = [jnp.broadcast_to(lr_ref[g], (bsz, n2)) for g in range(gb)]
    lis = [jnp.broadcast_to(li_ref[g], (bsz, n2)) for g in range(gb)]

    def step(i, carry):
        rf = pl.multiple_of(i * bsz, bsz)
        rb = pl.multiple_of((nc - 1 - i) * bsz, bsz)
        out = []
        for g in range(gb):
            sre, sim = carry[2 * g], carry[2 * g + 1]
            st = jnp.concatenate([sre, sim], axis=-1).astype(BF16)
            s0_ref[g, pl.ds(rf, bsz), :] = st
            s1_ref[g, pl.ds(rb, bsz), :] = st
            c = jnp.where(is_fwd, con_ref[g, pl.ds(rf, bsz), :], con_ref[g, pl.ds(rb, bsz), :])
            nre = lrs[g] * sre - lis[g] * sim + c[:, :n2]
            nim = lrs[g] * sim + lis[g] * sre + c[:, n2:]
            out += [nre, nim]
        return tuple(out)

    zero = jnp.zeros((bsz, n2), F32)
    lax.fori_loop(0, nc, step, (zero,) * (2 * gb))

    rows = nc * bsz
    lane_all = lax.broadcasted_iota(jnp.int32, (rows, 2 * n2), 1)
    sel = (lane_all % n2) < (n2 // 2)
    for g in range(gb):
        xs = jnp.where(sel, s0_ref[g], s1_ref[g])
        y = (jnp.dot(u_ref[g], mi_ref[g], preferred_element_type=F32)
             + jnp.dot(xs, cm_ref[g], preferred_element_type=F32))
        y_ref[g] = jax.nn.gelu(y).astype(BF16)


def _s5_mix(ut, mi, bm, cm, lr, li, nc, bsz):
    g, rows, tp = ut.shape
    gb = 2
    n4 = bm.shape[-1]
    blk = lambda a: pl.BlockSpec((gb,) + a.shape[1:], lambda j: (j,) + (0,) * (a.ndim - 1))
    return pl.pallas_call(
        functools.partial(_s5_body, gb=gb, nc=nc, bsz=bsz),
        grid=(g // gb,),
        in_specs=[blk(ut), blk(mi), blk(bm), blk(cm), blk(lr), blk(li)],
        out_specs=blk(ut),
        out_shape=jax.ShapeDtypeStruct(ut.shape, BF16),
        scratch_shapes=[pltpu.VMEM((gb, rows, n4), F32),
                        pltpu.VMEM((gb, rows, n4), BF16),
                        pltpu.VMEM((gb, rows, n4), BF16)],
        compiler_params=_cparams(("arbitrary",)),
        name="s5_mix",
    )(ut, mi, bm, cm, lr, li)


def _dn_body(q_ref, k_ref, v_ref, cq_ref, ck_ref, cv_ref, gates_ref, gt_ref, o_ref,
             pad_ref, qs_ref, ks_ref, vs_ref, bcol_ref, gcol_ref, grow_ref,
             mq_ref, n_ref, el_ref, *, seq, ch, nh):
    hd = q_ref.shape[-1]
    h = pl.program_id(1)
    nch = seq // ch
    na_par = 8
    halo = 8
    rb = 512
    nrb = seq // rb
    ng = gates_ref.shape[-1]

    pad_ref[pl.ds(0, halo), :] = jnp.zeros((halo, hd), F32)
    pad_ref[pl.ds(halo + seq, halo), :] = jnp.zeros((halo, hd), F32)

    def conv_into(src_ref, w_ref, dst_ref, mode):
        def fill(i, _):
            r0 = pl.multiple_of(i * rb, rb)
            pad_ref[pl.ds(halo + r0, rb), :] = src_ref[0, pl.ds(r0, rb), :].astype(F32)
            return 0
        lax.fori_loop(0, nrb, fill, 0)

        def blk(i, _):
            r0 = pl.multiple_of(i * rb, rb)
            acc = jnp.zeros((rb, hd), F32)
            for j in range(CONV_K):
                acc = acc + pad_ref[pl.ds(halo + r0 + (j - CONV_K // 2), rb), :] * w_ref[j:j + 1, :]
            a = acc * jax.nn.sigmoid(acc)
            if mode == "q":
                a = a * lax.rsqrt(jnp.sum(a * a, axis=-1, keepdims=True) + NORM_EPS) * (hd ** -0.5)
            elif mode == "k":
                a = a * lax.rsqrt(jnp.sum(a * a, axis=-1, keepdims=True) + NORM_EPS)
            dst_ref[pl.ds(r0, rb), :] = a.astype(dst_ref.dtype)
            return 0
        lax.fori_loop(0, nrb, blk, 0)

    conv_into(q_ref, cq_ref, qs_ref, "q")
    conv_into(k_ref, ck_ref, ks_ref, "k")
    conv_into(v_ref, cv_ref, vs_ref, "v")

    cols = (h, nh + h, 2 * nh + h, 3 * nh + h)
    gb = 512
    sel_c = lax.broadcasted_iota(jnp.int32, (gb, ng), 1)

    def col_of(x, col):
        return jnp.broadcast_to(jnp.sum(jnp.where(sel_c == col, x, 0.0), axis=-1, keepdims=True),
                                (gb, hd))

    def gate_blk(i, _):
        r0 = pl.multiple_of(i * gb, gb)
        gt = gates_ref[0, pl.ds(r0, gb), :]
        for d in range(2):
            bcol_ref[d, pl.ds(r0, gb), :] = col_of(gt, cols[d])
            gcol_ref[d, pl.ds(r0, gb), :] = col_of(gt, cols[2 + d])
        return 0
    lax.fori_loop(0, seq // gb, gate_blk, 0)

    ii = lax.broadcasted_iota(jnp.int32, (ch, ch), 0)
    jj = lax.broadcasted_iota(jnp.int32, (ch, ch), 1)
    for d in range(2):
        grow_ref[d] = gt_ref[0, cols[2 + d]]

    o_ref[...] = jnp.zeros(o_ref.shape, F32)
    masks = (ii >= jj, ii <= jj)
    stricts = (ii > jj, ii < jj)
    last_rows = (ch - 1, 0)
    eye = (ii == jj).astype(F32)
    same_blk = (ii // DN_SUB) == (jj // DN_SUB)
    n_sub_lvl = int(math.log2(DN_SUB))
    n_blk_lvl = int(math.log2(ch // DN_SUB))

    def prep_load(c, d):
        r0 = pl.multiple_of(c * ch, ch)
        return (d, qs_ref[pl.ds(r0, ch), :], ks_ref[pl.ds(r0, ch), :], vs_ref[pl.ds(r0, ch), :],
                bcol_ref[d, pl.ds(r0, ch), :], gcol_ref[d, pl.ds(r0, ch), :],
                grow_ref[d, pl.ds(c, 1), :])

    def prep_compute(loaded, tick):
        chains = []
        for v in loaded:
            both = _bdot_nt(jnp.concatenate([v[2], v[1]], axis=0), v[2])
            chains.append(dict(d=v[0], kk=both[:ch], qk=both[ch:]))
        tick()
        for cn, v in zip(chains, loaded):
            d, qc, kc, vc, beta, gc, gr = v
            kf, qf, vc = kc.astype(F32), qc.astype(F32), vc.astype(F32)
            mask = masks[d]
            decay = jnp.where(mask, jnp.exp(jnp.where(mask, gc[:, :ch] - gr, 0.0)), 0.0)
            a = jnp.where(stricts[d], beta[:, :ch] * cn["kk"] * decay, 0.0)
            a_d = jnp.where(same_blk, a, 0.0)
            egc = jnp.exp(gc)
            g_last = gc[last_rows[d]:last_rows[d] + 1, :]
            cn.update(a_d=a_d, dinv=eye - a_d,
                      rhs=jnp.concatenate([kf * (beta * egc), vc * beta, a - a_d], axis=-1),
                      attn=jnp.where(mask, cn["qk"] * decay, 0.0), qd=qf * egc,
                      kt=kf * jnp.exp(g_last - gc), el=jnp.exp(g_last))
        for cn in chains:
            cn["p"] = _bdot(cn["a_d"], cn["a_d"])
        tick()
        for lvl in range(1, n_sub_lvl):
            for cn in chains:
                if lvl + 1 < n_sub_lvl:
                    both = _bdot(jnp.concatenate([cn["dinv"], cn["p"]], axis=0), cn["p"])
                    cn["dinv"] = cn["dinv"] + both[:ch]
                    cn["p"] = both[ch:]
                else:
                    cn["dinv"] = cn["dinv"] + _bdot(cn["dinv"], cn["p"])
            tick()
        for cn in chains:
            first = _bdot(cn["dinv"], cn["rhs"])
            cn["sol"] = first[:, :2 * hd]
            cn["pw"] = first[:, 2 * hd:]
        tick()
        for lvl in range(n_blk_lvl):
            for cn in chains:
                upd = _bdot(cn["pw"], cn["sol"])
                cn["sol"] = cn["sol"] - upd if lvl == 0 else cn["sol"] + upd
                if lvl + 1 < n_blk_lvl:
                    cn["pw"] = _bdot(cn["pw"], cn["pw"])
            tick()
        for cn in chains:
            cn["kta"] = jnp.concatenate([cn["kt"].T, cn["attn"]], axis=0)
        tick()
        for cn in chains:
            cn["x"] = _bdot(cn["kta"], cn["sol"])
        tick()
        outs = []
        for cn in chains:
            x = cn["x"]
            outs.append((jnp.concatenate([-x[:hd, :hd], cn["qd"] - x[hd:, :hd]], axis=0).astype(BF16),
                         x[:hd, hd:], x[hd:, hd:], cn["el"]))
        return outs

    def prep_store(c, d, out):
        r0 = pl.multiple_of(c * ch, ch)
        mq_ref[d, c] = out[0]
        n_ref[d, c] = out[1]
        o_ref[0, pl.ds(r0, ch), :] += out[2]
        el_ref[d, pl.ds(c, 1), :] = out[3]

    n_iter = nch // na_par

    def chunk_of(i, j, d):
        return i * na_par + j if d == 0 else nch - 1 - (i * na_par + j)

    def prep_set(i, tick):
        todo = [(chunk_of(i, j, d), d) for j in range(na_par) for d in range(2)]
        outs = prep_compute([prep_load(c, d) for c, d in todo], tick)
        for (c, d), out in zip(todo, outs):
            prep_store(c, d, out)


    def state_step(i, j, states):
        new = []
        for d in range(2):
            c = chunk_of(i, j, d)
            r0 = pl.multiple_of(c * ch, ch)
            r = jnp.dot(mq_ref[d, c], states[d].astype(BF16), preferred_element_type=F32)
            o_ref[0, pl.ds(r0, ch), :] += r[hd:]
            new.append(states[d] * el_ref[d, pl.ds(c, 1), :] + r[:hd] + n_ref[d, c])
        return new

    prep_set(0, lambda: None)

    def pipelined(i, carry):
        box = dict(states=list(carry), j=0)

        def tick():
            if box["j"] < na_par:
                box["states"] = state_step(i - 1, box["j"], box["states"])
                box["j"] += 1
        prep_set(i, tick)
        while box["j"] < na_par:
            tick()
        return tuple(box["states"])

    z = jnp.zeros((hd, hd), F32)
    states = lax.fori_loop(1, n_iter, pipelined, (z, z))
    states = list(states)
    for j in range(na_par):
        states = state_step(n_iter - 1, j, states)


def _delta_net(qkv, conv_w, gates, nh):
    gates_t = gates.transpose(0, 2, 1).reshape(gates.shape[0], gates.shape[2], -1, DN_CHUNK)
    bsz, seq, _ = qkv.shape
    hd = DN_HEAD_DIM
    ch = DN_CHUNK
    nch = seq // ch
    ng = gates.shape[-1]
    col = lambda off: pl.BlockSpec((1, seq, hd), lambda b, h: (b, 0, off + h))
    cw = lambda off: pl.BlockSpec((CONV_K, hd), lambda b, h: (0, off + h))
    return pl.pallas_call(
        functools.partial(_dn_body, seq=seq, ch=ch, nh=nh),
        grid=(bsz, nh),
        in_specs=[col(0), col(nh), col(2 * nh), cw(0), cw(nh), cw(2 * nh),
                  pl.BlockSpec((1, seq, ng), lambda b, h: (b, 0, 0)),
                  pl.BlockSpec((1, ng, nch, ch), lambda b, h: (b, 0, 0, 0))],
        out_specs=pl.BlockSpec((1, seq, hd), lambda b, h: (b, 0, h)),
        out_shape=jax.ShapeDtypeStruct((bsz, seq, nh * hd), F32),
        scratch_shapes=[pltpu.VMEM((seq + 16, hd), F32),
                        pltpu.VMEM((seq, hd), BF16),
                        pltpu.VMEM((seq, hd), BF16),
                        pltpu.VMEM((seq, hd), BF16),
                        pltpu.VMEM((2, seq, hd), F32),
                        pltpu.VMEM((2, seq, hd), F32),
                        pltpu.VMEM((2, nch, ch), F32),
                        pltpu.VMEM((2, nch, hd + ch, hd), BF16),
                        pltpu.VMEM((2, nch, hd, hd), F32),
                        pltpu.VMEM((2, nch, hd), F32)],
        compiler_params=_cparams(("arbitrary", "arbitrary")),
        name="delta_net",
    )(qkv, qkv, qkv, conv_w, conv_w, conv_w, gates, gates_t)


def _out_body(x_ref, mod_ref, yt_ref, za_ref, o_ref, zb_ref, gw_ref, gbias_ref, dnw_ref,
              wo_ref, fw_ref, out_ref, slab_ref, *, nh, t_len):
    bsz, tt, d = x_ref.shape
    rows = bsz * tt
    d_ssm = za_ref.shape[-1]
    hd = o_ref.shape[-1] // nh
    n_tile = d_ssm // 128
    g_per_tile = 128 // SSM_GROUP

    mix = jnp.zeros((rows, d), F32)
    for hh in range(nh):
        o = o_ref[:, :, hh * hd:(hh + 1) * hd].reshape(rows, hd)
        zb = zb_ref[:, :, hh * hd:(hh + 1) * hd].reshape(rows, hd).astype(F32)
        o = o * lax.rsqrt(jnp.mean(o * o, axis=-1, keepdims=True) + NORM_EPS) * dnw_ref[...]
        yb = (o * (zb * jax.nn.sigmoid(zb))).astype(BF16)
        mix = mix + jnp.dot(yb, wo_ref[pl.ds(d_ssm + hh * hd, hd), :], preferred_element_type=F32)

    grp = _lane_group((bsz, 128))
    for cc in range(tt // t_len):
        for j in range(n_tile):
            for a in range(t_len // g_per_tile):
                ys = [yt_ref[j * g_per_tile + g8, cc, :, a * 128:(a + 1) * 128].astype(F32)
                      for g8 in range(g_per_tile)]
                for tp in range(g_per_tile):
                    tile = jnp.zeros((bsz, 128), F32)
                    for g8 in range(g_per_tile):
                        sh = (SSM_GROUP * (g8 - tp)) % 128
                        tile = jnp.where(grp == g8, pltpu.roll(ys[g8], sh, 1) if sh else ys[g8], tile)
                    t = a * g_per_tile + tp
                    slab_ref[j, pl.ds(cc * t_len + t, bsz, stride=tt), :] = tile
    ysn = jnp.concatenate([slab_ref[j] for j in range(n_tile)], axis=-1).astype(BF16)

    glu = jnp.dot(ysn, gw_ref[...], preferred_element_type=F32) + gbias_ref[...]
    za = za_ref[...].reshape(rows, d_ssm).astype(F32)
    ya = glu[:, :d_ssm] * jax.nn.sigmoid(glu[:, d_ssm:]) * (za * jax.nn.sigmoid(za))
    mix = mix + jnp.dot(ya.astype(BF16), wo_ref[pl.ds(0, d_ssm), :], preferred_element_type=F32)
    r = x_ref[...] + mod_ref[:, 2:3, :] * mix.reshape(bsz, tt, d)
    out_ref[...] = r * lax.rsqrt(jnp.mean(r * r, axis=-1, keepdims=True) + NORM_EPS) * fw_ref[...]


def _out_proj(x, mod3, yt, za, o, zb, glu_w, glu_b, dn_norm_w, w_out, final_norm_w, nh, t_len):
    bsz, seq, d = x.shape
    tt = 64
    g = yt.shape[0]
    full = lambda a: pl.BlockSpec(a.shape, lambda i: (0,) * a.ndim)
    row = lambda a: pl.BlockSpec((bsz, tt, a.shape[-1]), lambda i: (0, i, 0))
    gw = glu_w.astype(BF16)
    gbias = glu_b.reshape(1, -1)
    dnw = dn_norm_w.reshape(1, -1)
    wo = w_out.astype(BF16)
    fw = final_norm_w.reshape(1, -1)
    return pl.pallas_call(
        functools.partial(_out_body, nh=nh, t_len=t_len),
        grid=(seq // tt,),
        in_specs=[row(x), full(mod3),
                  pl.BlockSpec((g, tt // t_len, bsz, yt.shape[-1]), lambda i: (0, i, 0, 0)),
                  row(za), row(o), row(zb),
                  full(gw), full(gbias), full(dnw), full(wo), full(fw)],
        out_specs=row(x),
        out_shape=jax.ShapeDtypeStruct(x.shape, F32),
        scratch_shapes=[pltpu.VMEM((za.shape[-1] // 128, bsz * tt, 128), F32)],
        compiler_params=_cparams(("arbitrary",)),
        name="out_proj",
    )(x, mod3, yt, za, o, zb, gw, gbias, dnw, wo, fw)


def _layer(x, c, ada_w, ada_b, norm_w, w_in, conv_w,
           dn_a_log_f, dn_dt_bias_f, dn_a_log_b, dn_dt_bias_b, dn_norm_w,
           s5f, s5b, s5_d, glu_w, glu_b, w_out, out_norm_w):
    bsz, seq, d = x.shape
    d_ssm = s5_d.shape[0]
    d_dn = w_out.shape[0] - d_ssm
    nh = d_dn // DN_HEAD_DIM
    n_gate = 4 * nh
    p = SSM_GROUP
    g = d_ssm // p
    t = S5_T
    nc = seq // t

    mod3 = _ada_mod(c, ada_w, ada_b).reshape(bsz, 3, d)
    zeros = jnp.zeros((2 * nh,), F32)
    gparams = jnp.stack([
        jnp.concatenate([zeros, dn_a_log_f.astype(F32), dn_a_log_b.astype(F32)]),
        jnp.concatenate([zeros, dn_dt_bias_f.astype(F32), dn_dt_bias_b.astype(F32)])])
    gparams = jnp.pad(gparams, ((0, 0), (0, 128 - n_gate)))
    ut, za, qkv, zb, gates = _in_proj(x, mod3, norm_w, w_in, gparams, d_ssm, d_dn, n_gate, t)

    def mats(prm):
        lam_re, lam_im, log_step, b_re, b_im, c_re, c_im = prm
        pr, pi, br, bi = _s5_powers(lam_re, lam_im, log_step, b_re, b_im, t + 1)
        return pr, pi, br, bi, c_re.astype(F32), c_im.astype(F32)
    mi, bm, cm, lr, li = _s5_matrices(mats(s5f), mats(s5b), s5_d, t)
    yt = _s5_mix(ut.reshape(g, nc * bsz, t * p), mi, bm, cm, lr, li, nc, bsz)
    yt = yt.reshape(g, nc, bsz, t * p)

    o = _delta_net(qkv, conv_w.astype(F32), gates, nh)

    return _out_proj(x, mod3, yt, za, o, zb, glu_w, glu_b, dn_norm_w, w_out, out_norm_w, nh, t)


def kernel(x, c, ada_w, ada_b, norm_w, w_in, conv_w, dn_a_log_f, dn_dt_bias_f, dn_a_log_b,
           dn_dt_bias_b, dn_norm_w, lam_re_f, lam_im_f, log_step_f, b_re_f, b_im_f, c_re_f, c_im_f,
           lam_re_b, lam_im_b, log_step_b, b_re_b, b_im_b, c_re_b, c_im_b, s5_d, glu_w, glu_b,
           w_out, final_norm_w):
    assert ada_w.shape[0] == 1, "single-layer block"
    s5f = (lam_re_f[0], lam_im_f[0], log_step_f[0], b_re_f[0], b_im_f[0], c_re_f[0], c_im_f[0])
    s5b = (lam_re_b[0], lam_im_b[0], log_step_b[0], b_re_b[0], b_im_b[0], c_re_b[0], c_im_b[0])
    return _layer(x, c, ada_w[0], ada_b[0], norm_w[0], w_in[0], conv_w[0],
                  dn_a_log_f[0], dn_dt_bias_f[0], dn_a_log_b[0], dn_dt_bias_b[0], dn_norm_w[0],
                  s5f, s5b, s5_d[0], glu_w[0], glu_b[0], w_out[0], final_norm_w)
```

```python
import functools
import math

import jax
import jax.numpy as jnp
from jax import lax
from jax.experimental import pallas as pl
from jax.experimental.pallas import tpu as pltpu

F32 = jnp.float32
BF16 = jnp.bfloat16
HIGHEST = lax.Precision.HIGHEST

NORM_EPS = 1e-6
SSM_GROUP = 16
SSM_STATE = 64
DN_HEAD_DIM = 128
CONV_K = 5
S5_T = 16
DN_CHUNK = 64
DN_SUB = 16
VMEM_LIMIT = 56 * 1024 * 1024


def _cparams(sem):
    return pltpu.CompilerParams(dimension_semantics=sem, vmem_limit_bytes=VMEM_LIMIT)


def _bdot(a, b):
    return jnp.dot(a.astype(BF16), b.astype(BF16), preferred_element_type=F32)


def _bdot_nt(a, b):
    return lax.dot_general(a.astype(BF16), b.astype(BF16), (((1,), (1,)), ((), ())),
                           preferred_element_type=F32)


def _mod_body(c_ref, w_ref, b_ref, o_ref):
    c = c_ref[...]
    o_ref[...] = _bdot(c * jax.nn.sigmoid(c), w_ref[...]) + b_ref[...]


def _ada_mod(c, ada_w, ada_b):
    bsz, d = c.shape
    n = ada_w.shape[1]
    tn = 1024
    return pl.pallas_call(
        _mod_body,
        grid=(n // tn,),
        in_specs=[pl.BlockSpec((bsz, d), lambda j: (0, 0)),
                  pl.BlockSpec((d, tn), lambda j: (0, j)),
                  pl.BlockSpec((1, tn), lambda j: (0, j))],
        out_specs=pl.BlockSpec((bsz, tn), lambda j: (0, j)),
        out_shape=jax.ShapeDtypeStruct((bsz, n), F32),
        compiler_params=_cparams(("arbitrary",)),
        name="ada_mod",
    )(c, ada_w, ada_b.reshape(1, n))


def _lane_group(shape):
    return lax.broadcasted_iota(jnp.int32, shape, len(shape) - 1) // SSM_GROUP


def _inproj_body(x_ref, mod_ref, nw_ref, wu_ref, wza_ref, wqkv_ref, wzb_ref, wg_ref, gp_ref,
                 ut_ref, za_ref, qkv_ref, zb_ref, g_ref, slab_ref, *, t_len, dn_chunk):
    bsz, tt, d = x_ref.shape
    rows = bsz * tt
    x = x_ref[...]
    ms = jnp.mean(x * x, axis=-1, keepdims=True)
    y = x * lax.rsqrt(ms + NORM_EPS) * nw_ref[...]
    h = (y * (1.0 + mod_ref[:, 1:2, :]) + mod_ref[:, 0:1, :]).reshape(rows, d).astype(BF16)

    u = jnp.dot(h, wu_ref[...], preferred_element_type=F32)
    n_tile = u.shape[-1] // 128
    g_per_tile = 128 // SSM_GROUP
    for j in range(n_tile):
        slab_ref[j] = u[:, j * 128:(j + 1) * 128]
    grp = _lane_group((bsz, 128))
    for cc in range(tt // t_len):
        for j in range(n_tile):
            xs = [slab_ref[j, pl.ds(cc * t_len + s, bsz, stride=tt), :] for s in range(t_len)]
            for g8 in range(g_per_tile):
                for a in range(t_len // g_per_tile):
                    tile = jnp.zeros((bsz, 128), F32)
                    for sp in range(g_per_tile):
                        src = xs[a * g_per_tile + sp]
                        sh = (SSM_GROUP * (sp - g8)) % 128
                        tile = jnp.where(grp == sp, pltpu.roll(src, sh, 1) if sh else src, tile)
                    ut_ref[j * g_per_tile + g8, cc, :, a * 128:(a + 1) * 128] = tile.astype(BF16)

    r = jnp.dot(h, wg_ref[...], preferred_element_type=F32)
    n_gate = g_ref.shape[-1]
    lane = lax.broadcasted_iota(jnp.int32, r.shape, 1)
    gl = -jnp.exp(gp_ref[0:1, :]) * jax.nn.softplus(r + gp_ref[1:2, :])
    pos = lax.broadcasted_iota(jnp.int32, r.shape, 0) % dn_chunk
    pre = gl
    suf = gl
    sh = 1
    while sh < dn_chunk:
        pre = pre + jnp.where(pos >= sh, pltpu.roll(pre, sh, 0), 0.0)
        suf = suf + jnp.where(pos < dn_chunk - sh, pltpu.roll(suf, rows - sh, 0), 0.0)
        sh *= 2
    gv = jnp.where(lane < n_gate // 2, jax.nn.sigmoid(r), jnp.where(lane < 3 * n_gate // 4, pre, suf))
    g_ref[...] = gv.reshape(bsz, tt, gv.shape[-1])[:, :, :n_gate]

    def proj(w_ref, out_ref):
        r = jnp.dot(h, w_ref[...], preferred_element_type=F32)
        out_ref[...] = r.reshape(bsz, tt, r.shape[-1])[:, :, :out_ref.shape[-1]].astype(out_ref.dtype)

    proj(wza_ref, za_ref)
    proj(wqkv_ref, qkv_ref)
    proj(wzb_ref, zb_ref)


def _in_proj(x, mod3, norm_w, w_in, gparams, d_ssm, d_dn, n_gate, t_len):
    bsz, seq, d = x.shape
    tt = DN_CHUNK
    o1 = d_ssm
    o2 = o1 + d_ssm
    o3 = o2 + 3 * d_dn
    o4 = o3 + d_dn
    wb = w_in.astype(BF16)
    wu, wza, wqkv, wzb = wb[:, :o1], wb[:, o1:o2], wb[:, o2:o3], wb[:, o3:o4]
    wg = jnp.pad(wb[:, o4:], ((0, 0), (0, 128 - n_gate)))
    g = d_ssm // SSM_GROUP
    full = lambda a: pl.BlockSpec(a.shape, lambda i: (0,) * a.ndim)
    row = lambda w: pl.BlockSpec((bsz, tt, w), lambda i: (0, i, 0))
    return pl.pallas_call(
        functools.partial(_inproj_body, t_len=t_len, dn_chunk=DN_CHUNK),
        grid=(seq // tt,),
        in_specs=[row(d), full(mod3), pl.BlockSpec((1, d), lambda i: (0, 0)),
                  full(wu), full(wza), full(wqkv), full(wzb), full(wg), full(gparams)],
        out_specs=[pl.BlockSpec((g, tt // t_len, bsz, t_len * SSM_GROUP), lambda i: (0, i, 0, 0)),
                   row(d_ssm), row(3 * d_dn), row(d_dn), row(n_gate)],
        out_shape=[jax.ShapeDtypeStruct((g, seq // t_len, bsz, t_len * SSM_GROUP), BF16),
                   jax.ShapeDtypeStruct((bsz, seq, d_ssm), BF16),
                   jax.ShapeDtypeStruct((bsz, seq, 3 * d_dn), BF16),
                   jax.ShapeDtypeStruct((bsz, seq, d_dn), BF16),
                   jax.ShapeDtypeStruct((bsz, seq, n_gate), F32)],
        scratch_shapes=[pltpu.VMEM((d_ssm // 128, bsz * tt, 128), F32)],
        compiler_params=_cparams(("arbitrary",)),
        name="in_proj",
    )(x, mod3, norm_w.reshape(1, d), wu, wza, wqkv, wzb, wg, gparams)


def _s5_powers(lam_re, lam_im, log_step, b_re, b_im, n_pow):
    dt = jnp.exp(log_step.astype(F32))[:, None]
    lr, li = lam_re.astype(F32), lam_im.astype(F32)
    mag = jnp.exp(lr * dt)
    ab_re, ab_im = mag * jnp.cos(li * dt), mag * jnp.sin(li * dt)
    nr, ni = ab_re - 1.0, ab_im
    den = lr * lr + li * li
    f_re = (nr * lr + ni * li) / den
    f_im = (ni * lr - nr * li) / den
    br, bi = b_re.astype(F32), b_im.astype(F32)
    bb_re = f_re[..., None] * br - f_im[..., None] * bi
    bb_im = f_re[..., None] * bi + f_im[..., None] * br
    tau = jnp.arange(n_pow, dtype=F32)[:, None, None]
    pmag = jnp.exp(tau * (lr * dt)[None])
    ang = tau * (li * dt)[None]
    return pmag * jnp.cos(ang), pmag * jnp.sin(ang), bb_re, bb_im


def _s5_matrices(pf, pb, s5_d, t_len):
    (pfr, pfi, bfr, bfi, cfr, cfi) = pf
    (pbr, pbi, bbr, bbi, cbr, cbi) = pb
    g, n, p = bfr.shape
    t = t_len
    ein = functools.partial(jnp.einsum, precision=HIGHEST)

    def lag_kernels(pr, pi, br, bi, cr, ci):
        xr = pr[:t, :, :, None] * br[None] - pi[:t, :, :, None] * bi[None]
        xi = pr[:t, :, :, None] * bi[None] + pi[:t, :, :, None] * br[None]
        return ein("gpn,tgnq->tgpq", cr, xr) - ein("gpn,tgnq->tgpq", ci, xi)

    kf = lag_kernels(pfr, pfi, bfr, bfi, cfr, cfi)
    kb = lag_kernels(pbr, pbi, bbr, bbi, cbr, cbi)
    dmat = jnp.eye(p, dtype=F32)[None] * s5_d.astype(F32).reshape(g, p)[:, :, None]
    k0 = kf[0] + kb[0] + dmat
    kall = jnp.concatenate([kb[1:][::-1], k0[None], kf[1:]], axis=0).astype(BF16)
    idx = jnp.arange(t)[None, :] - jnp.arange(t)[:, None] + (t - 1)
    mi = kall[idx]
    mi = mi.transpose(2, 0, 4, 1, 3).reshape(g, t * p, t * p)

    def in_map(pr, pi, br, bi, order):
        prs, pis = pr[order], pi[order]
        re = prs[..., None] * br[None] - pis[..., None] * bi[None]
        im = prs[..., None] * bi[None] + pis[..., None] * br[None]
        f = lambda a: a.astype(BF16).transpose(1, 0, 3, 2).reshape(g, t * p, n)
        return f(re), f(im)

    s_idx = jnp.arange(t)
    bfre, bfim = in_map(pfr, pfi, bfr, bfi, t - 1 - s_idx)
    bbre, bbim = in_map(pbr, pbi, bbr, bbi, s_idx)
    bm = jnp.concatenate([bfre, bbre, bfim, bbim], axis=-1)

    def out_map(pr, pi, cr, ci, order):
        prs, pis = pr[order], pi[order]
        wre = cr[None] * prs[:, :, None, :] - ci[None] * pis[:, :, None, :]
        wim = -(cr[None] * pis[:, :, None, :] + ci[None] * prs[:, :, None, :])
        f = lambda a: a.astype(BF16).transpose(1, 3, 0, 2).reshape(g, n, t * p)
        return f(wre), f(wim)

    cfre, cfim = out_map(pfr, pfi, cfr, cfi, s_idx + 1)
    cbre, cbim = out_map(pbr, pbi, cbr, cbi, t - s_idx)
    cm = jnp.concatenate([cfre, cbre, cfim, cbim], axis=1)
    lr = jnp.concatenate([pfr[t], pbr[t]], axis=-1)[:, None, :]
    li = jnp.concatenate([pfi[t], pbi[t]], axis=-1)[:, None, :]
    return mi, bm, cm, lr, li


def _s5_body(u_ref, mi_ref, bm_ref, cm_ref, lr_ref, li_ref, y_ref, con_ref, s0_ref, s1_ref,
             *, gb, nc, bsz):
    n2 = lr_ref.shape[-1]
    for g in range(gb):
        con_ref[g] = jnp.dot(u_ref[g], bm_ref[g], preferred_element_type=F32)
    lane = lax.broadcasted_iota(jnp.int32, (bsz, 2 * n2), 1)
    is_fwd = (lane % n2) < (n2 // 2)
    lrs = [jnp.broadcast_to(lr_ref[g], (bsz, n2)) for g in range(gb)]
    lis = [jnp.broadcast_to(li_ref[g], (bsz, n2)) for g in range(gb)]

    def step(i, carry):
        rf = pl.multiple_of(i * bsz, bsz)
        rb = pl.multiple_of((nc - 1 - i) * bsz, bsz)
        out = []
        for g in range(gb):
            sre, sim = carry[2 * g], carry[2 * g + 1]
            st = jnp.concatenate([sre, sim], axis=-1).astype(BF16)
            s0_ref[g, pl.ds(rf, bsz), :] = st
            s1_ref[g, pl.ds(rb, bsz), :] = st
            c = jnp.where(is_fwd, con_ref[g, pl.ds(rf, bsz), :], con_ref[g, pl.ds(rb, bsz), :])
            nre = lrs[g] * sre - lis[g] * sim + c[:, :n2]
            nim = lrs[g] * sim + lis[g] * sre + c[:, n2:]
            out += [nre, nim]
        return tuple(out)

    zero = jnp.zeros((bsz, n2), F32)
    lax.fori_loop(0, nc, step, (zero,) * (2 * gb))

    rows = nc * bsz
    lane_all = lax.broadcasted_iota(jnp.int32, (rows, 2 * n2), 1)
    sel = (lane_all % n2) < (n2 // 2)
    for g in range(gb):
        xs = jnp.where(sel, s0_ref[g], s1_ref[g])
        y = (jnp.dot(u_ref[g], mi_ref[g], preferred_element_type=F32)
             + jnp.dot(xs, cm_ref[g], preferred_element_type=F32))
        y_ref[g] = jax.nn.gelu(y).astype(BF16)


def _s5_mix(ut, mi, bm, cm, lr, li, nc, bsz):
    g, rows, tp = ut.shape
    gb = 2
    n4 = bm.shape[-1]
    blk = lambda a: pl.BlockSpec((gb,) + a.shape[1:], lambda j: (j,) + (0,) * (a.ndim - 1))
    return pl.pallas_call(
        functools.partial(_s5_body, gb=gb, nc=nc, bsz=bsz),
        grid=(g // gb,),
        in_specs=[blk(ut), blk(mi), blk(bm), blk(cm), blk(lr), blk(li)],
        out_specs=blk(ut),
        out_shape=jax.ShapeDtypeStruct(ut.shape, BF16),
        scratch_shapes=[pltpu.VMEM((gb, rows, n4), F32),
                        pltpu.VMEM((gb, rows, n4), BF16),
                        pltpu.VMEM((gb, rows, n4), BF16)],
        compiler_params=_cparams(("arbitrary",)),
        name="s5_mix",
    )(ut, mi, bm, cm, lr, li)


def _dn_body(q_ref, k_ref, v_ref, cq_ref, ck_ref, cv_ref, gt_ref, o_ref,
             pad_ref, qs_ref, ks_ref, vs_ref, mq_ref, n_ref, el_ref, *, seq, ch, nh):
    hd = q_ref.shape[-1]
    h = pl.program_id(1)
    nch = seq // ch
    na_par = 8
    halo = 8
    rb = 512
    nrb = seq // rb

    pad_ref[pl.ds(0, halo), :] = jnp.zeros((halo, hd), F32)
    pad_ref[pl.ds(halo + seq, halo), :] = jnp.zeros((halo, hd), F32)

    def conv_into(src_ref, w_ref, dst_ref, mode):
        def fill(i, _):
            r0 = pl.multiple_of(i * rb, rb)
            pad_ref[pl.ds(halo + r0, rb), :] = src_ref[0, pl.ds(r0, rb), :].astype(F32)
            return 0
        lax.fori_loop(0, nrb, fill, 0)

        def blk(i, _):
            r0 = pl.multiple_of(i * rb, rb)
            acc = jnp.zeros((rb, hd), F32)
            for j in range(CONV_K):
                acc = acc + pad_ref[pl.ds(halo + r0 + (j - CONV_K // 2), rb), :] * w_ref[j:j + 1, :]
            a = acc * jax.nn.sigmoid(acc)
            if mode == "q":
                a = a * lax.rsqrt(jnp.sum(a * a, axis=-1, keepdims=True) + NORM_EPS) * (hd ** -0.5)
            elif mode == "k":
                a = a * lax.rsqrt(jnp.sum(a * a, axis=-1, keepdims=True) + NORM_EPS)
            dst_ref[pl.ds(r0, rb), :] = a.astype(dst_ref.dtype)
            return 0
        lax.fori_loop(0, nrb, blk, 0)

    conv_into(q_ref, cq_ref, qs_ref, "q")
    conv_into(k_ref, ck_ref, ks_ref, "k")
    conv_into(v_ref, cv_ref, vs_ref, "v")

    cols = (h, nh + h, 2 * nh + h, 3 * nh + h)

    ii = lax.broadcasted_iota(jnp.int32, (ch, ch), 0)
    jj = lax.broadcasted_iota(jnp.int32, (ch, ch), 1)

    def col_forms(row_a, row_b):
        assert 2 * ch == hd
        t = jnp.broadcast_to(jnp.concatenate([row_a, row_b], axis=-1), (hd, hd)).T
        return t[:ch], t[ch:]

    o_ref[...] = jnp.zeros(o_ref.shape, F32)
    masks = (ii >= jj, ii <= jj)
    stricts = (ii > jj, ii < jj)
    last_rows = (ch - 1, 0)
    eye = (ii == jj).astype(F32)
    same_blk = (ii // DN_SUB) == (jj // DN_SUB)
    n_sub_lvl = int(math.log2(DN_SUB))
    n_blk_lvl = int(math.log2(ch // DN_SUB))

    def prep_load(c, d):
        r0 = pl.multiple_of(c * ch, ch)
        return (d, qs_ref[pl.ds(r0, ch), :], ks_ref[pl.ds(r0, ch), :], vs_ref[pl.ds(r0, ch), :],
                gt_ref[0, cols[d], pl.ds(c, 1), :], gt_ref[0, cols[2 + d], pl.ds(c, 1), :])

    def prep_compute(loaded, tick):
        chains = []
        for v in loaded:
            both = _bdot_nt(jnp.concatenate([v[2], v[1]], axis=0), v[2])
            chains.append(dict(d=v[0], kk=both[:ch], qk=both[ch:]))
        tick()
        for cn, v in zip(chains, loaded):
            d, qc, kc, vc, br, gr = v
            beta, gc = col_forms(br, gr)
            kf, qf, vc = kc.astype(F32), qc.astype(F32), vc.astype(F32)
            mask = masks[d]
            decay = jnp.where(mask, jnp.exp(jnp.where(mask, gc[:, :ch] - gr, 0.0)), 0.0)
            a = jnp.where(stricts[d], beta[:, :ch] * cn["kk"] * decay, 0.0)
            a_d = jnp.where(same_blk, a, 0.0)
            egc = jnp.exp(gc)
            g_last = gc[last_rows[d]:last_rows[d] + 1, :]
            cn.update(a_d=a_d, dinv=eye - a_d,
                      rhs=jnp.concatenate([kf * (beta * egc), vc * beta, a - a_d], axis=-1),
                      attn=jnp.where(mask, cn["qk"] * decay, 0.0), qd=qf * egc,
                      kt=kf * jnp.exp(g_last - gc), el=jnp.exp(g_last))
        for cn in chains:
            cn["p"] = _bdot(cn["a_d"], cn["a_d"])
        tick()
        for lvl in range(1, n_sub_lvl):
            for cn in chains:
                if lvl + 1 < n_sub_lvl:
                    both = _bdot(jnp.concatenate([cn["dinv"], cn["p"]], axis=0), cn["p"])
                    cn["dinv"] = cn["dinv"] + both[:ch]
                    cn["p"] = both[ch:]
                else:
                    cn["dinv"] = cn["dinv"] + _bdot(cn["dinv"], cn["p"])
            tick()
        for cn in chains:
            first = _bdot(cn["dinv"], cn["rhs"])
            cn["sol"] = first[:, :2 * hd]
            cn["pw"] = first[:, 2 * hd:]
        tick()
        for lvl in range(n_blk_lvl):
            for cn in chains:
                upd = _bdot(cn["pw"], cn["sol"])
                cn["sol"] = cn["sol"] - upd if lvl == 0 else cn["sol"] + upd
                if lvl + 1 < n_blk_lvl:
                    cn["pw"] = _bdot(cn["pw"], cn["pw"])
            tick()
        for cn in chains:
            cn["kta"] = jnp.concatenate([cn["kt"].T, cn["attn"]], axis=0)
        tick()
        for cn in chains:
            cn["x"] = _bdot(cn["kta"], cn["sol"])
        tick()
        outs = []
        for cn in chains:
            x = cn["x"]
            outs.append((jnp.concatenate([-x[:hd, :hd], cn["qd"] - x[hd:, :hd]], axis=0).astype(BF16),
                         x[:hd, hd:], x[hd:, hd:], cn["el"]))
        return outs

    def prep_store(c, d, out):
        r0 = pl.multiple_of(c * ch, ch)
        mq_ref[d, c] = out[0]
        n_ref[d, c] = out[1]
        o_ref[0, pl.ds(r0, ch), :] += out[2]
        el_ref[d, pl.ds(c, 1), :] = out[3]

    n_iter = nch // na_par

    def chunk_of(i, j, d):
        return i * na_par + j if d == 0 else nch - 1 - (i * na_par + j)

    def prep_set(i, tick):
        todo = [(chunk_of(i, j, d), d) for j in range(na_par) for d in range(2)]
        outs = prep_compute([prep_load(c, d) for c, d in todo], tick)
        for (c, d), out in zip(todo, outs):
            prep_store(c, d, out)


    def state_step(i, j, states):
        new = []
        for d in range(2):
            c = chunk_of(i, j, d)
            r0 = pl.multiple_of(c * ch, ch)
            r = jnp.dot(mq_ref[d, c], states[d].astype(BF16), preferred_element_type=F32)
            o_ref[0, pl.ds(r0, ch), :] += r[hd:]
            new.append(states[d] * el_ref[d, pl.ds(c, 1), :] + r[:hd] + n_ref[d, c])
        return new

    prep_set(0, lambda: None)

    def pipelined(i, carry):
        box = dict(states=list(carry), j=0)

        def tick():
            if box["j"] < na_par:
                box["states"] = state_step(i - 1, box["j"], box["states"])
                box["j"] += 1
        prep_set(i, tick)
        while box["j"] < na_par:
            tick()
        return tuple(box["states"])

    z = jnp.zeros((hd, hd), F32)
    states = lax.fori_loop(1, n_iter, pipelined, (z, z))
    states = list(states)
    for j in range(na_par):
        states = state_step(n_iter - 1, j, states)


def _delta_net(qkv, conv_w, gates, nh):
    gates_t = gates.transpose(0, 2, 1).reshape(gates.shape[0], gates.shape[2], -1, DN_CHUNK)
    bsz, seq, _ = qkv.shape
    hd = DN_HEAD_DIM
    ch = DN_CHUNK
    nch = seq // ch
    ng = gates.shape[-1]
    col = lambda off: pl.BlockSpec((1, seq, hd), lambda b, h: (b, 0, off + h))
    cw = lambda off: pl.BlockSpec((CONV_K, hd), lambda b, h: (0, off + h))
    return pl.pallas_call(
        functools.partial(_dn_body, seq=seq, ch=ch, nh=nh),
        grid=(bsz, nh),
        in_specs=[col(0), col(nh), col(2 * nh), cw(0), cw(nh), cw(2 * nh),
                  pl.BlockSpec((1, ng, nch, ch), lambda b, h: (b, 0, 0, 0))],
        out_specs=pl.BlockSpec((1, seq, hd), lambda b, h: (b, 0, h)),
        out_shape=jax.ShapeDtypeStruct((bsz, seq, nh * hd), F32),
        scratch_shapes=[pltpu.VMEM((seq + 16, hd), F32),
                        pltpu.VMEM((seq, hd), BF16),
                        pltpu.VMEM((seq, hd), BF16),
                        pltpu.VMEM((seq, hd), BF16),
                        pltpu.VMEM((2, nch, hd + ch, hd), BF16),
                        pltpu.VMEM((2, nch, hd, hd), F32),
                        pltpu.VMEM((2, nch, hd), F32)],
        compiler_params=_cparams(("arbitrary", "arbitrary")),
        name="delta_net",
    )(qkv, qkv, qkv, conv_w, conv_w, conv_w, gates_t)


def _out_body(x_ref, mod_ref, yt_ref, za_ref, o_ref, zb_ref, gw_ref, gbias_ref, dnw_ref,
              wo_ref, fw_ref, out_ref, slab_ref, *, nh, t_len):
    bsz, tt, d = x_ref.shape
    rows = bsz * tt
    d_ssm = za_ref.shape[-1]
    hd = o_ref.shape[-1] // nh
    n_tile = d_ssm // 128
    g_per_tile = 128 // SSM_GROUP

    mix = jnp.zeros((rows, d), F32)
    for hh in range(nh):
        o = o_ref[:, :, hh * hd:(hh + 1) * hd].reshape(rows, hd)
        zb = zb_ref[:, :, hh * hd:(hh + 1) * hd].reshape(rows, hd).astype(F32)
        o = o * lax.rsqrt(jnp.mean(o * o, axis=-1, keepdims=True) + NORM_EPS) * dnw_ref[...]
        yb = (o * (zb * jax.nn.sigmoid(zb))).astype(BF16)
        mix = mix + jnp.dot(yb, wo_ref[pl.ds(d_ssm + hh * hd, hd), :], preferred_element_type=F32)

    grp = _lane_group((bsz, 128))
    for cc in range(tt // t_len):
        for j in range(n_tile):
            for a in range(t_len // g_per_tile):
                ys = [yt_ref[j * g_per_tile + g8, cc, :, a * 128:(a + 1) * 128].astype(F32)
                      for g8 in range(g_per_tile)]
                for tp in range(g_per_tile):
                    tile = jnp.zeros((bsz, 128), F32)
                    for g8 in range(g_per_tile):
                        sh = (SSM_GROUP * (g8 - tp)) % 128
                        tile = jnp.where(grp == g8, pltpu.roll(ys[g8], sh, 1) if sh else ys[g8], tile)
                    t = a * g_per_tile + tp
                    slab_ref[j, pl.ds(cc * t_len + t, bsz, stride=tt), :] = tile
    ysn = jnp.concatenate([slab_ref[j] for j in range(n_tile)], axis=-1).astype(BF16)

    glu = jnp.dot(ysn, gw_ref[...], preferred_element_type=F32) + gbias_ref[...]
    za = za_ref[...].reshape(rows, d_ssm).astype(F32)
    ya = glu[:, :d_ssm] * jax.nn.sigmoid(glu[:, d_ssm:]) * (za * jax.nn.sigmoid(za))
    mix = mix + jnp.dot(ya.astype(BF16), wo_ref[pl.ds(0, d_ssm), :], preferred_element_type=F32)
    r = x_ref[...] + mod_ref[:, 2:3, :] * mix.reshape(bsz, tt, d)
    out_ref[...] = r * lax.rsqrt(jnp.mean(r * r, axis=-1, keepdims=True) + NORM_EPS) * fw_ref[...]


def _out_proj(x, mod3, yt, za, o, zb, glu_w, glu_b, dn_norm_w, w_out, final_norm_w, nh, t_len):
    bsz, seq, d = x.shape
    tt = 64
    g = yt.shape[0]
    full = lambda a: pl.BlockSpec(a.shape, lambda i: (0,) * a.ndim)
    row = lambda a: pl.BlockSpec((bsz, tt, a.shape[-1]), lambda i: (0, i, 0))
    gw = glu_w.astype(BF16)
    gbias = glu_b.reshape(1, -1)
    dnw = dn_norm_w.reshape(1, -1)
    wo = w_out.astype(BF16)
    fw = final_norm_w.reshape(1, -1)
    return pl.pallas_call(
        functools.partial(_out_body, nh=nh, t_len=t_len),
        grid=(seq // tt,),
        in_specs=[row(x), full(mod3),
                  pl.BlockSpec((g, tt // t_len, bsz, yt.shape[-1]), lambda i: (0, i, 0, 0)),
                  row(za), row(o), row(zb),
                  full(gw), full(gbias), full(dnw), full(wo), full(fw)],
        out_specs=row(x),
        out_shape=jax.ShapeDtypeStruct(x.shape, F32),
        scratch_shapes=[pltpu.VMEM((za.shape[-1] // 128, bsz * tt, 128), F32)],
        compiler_params=_cparams(("arbitrary",)),
        name="out_proj",
    )(x, mod3, yt, za, o, zb, gw, gbias, dnw, wo, fw)


def _layer(x, c, ada_w, ada_b, norm_w, w_in, conv_w,
           dn_a_log_f, dn_dt_bias_f, dn_a_log_b, dn_dt_bias_b, dn_norm_w,
           s5f, s5b, s5_d, glu_w, glu_b, w_out, out_norm_w):
    bsz, seq, d = x.shape
    d_ssm = s5_d.shape[0]
    d_dn = w_out.shape[0] - d_ssm
    nh = d_dn // DN_HEAD_DIM
    n_gate = 4 * nh
    p = SSM_GROUP
    g = d_ssm // p
    t = S5_T
    nc = seq // t

    mod3 = _ada_mod(c, ada_w, ada_b).reshape(bsz, 3, d)
    zeros = jnp.zeros((2 * nh,), F32)
    gparams = jnp.stack([
        jnp.concatenate([zeros, dn_a_log_f.astype(F32), dn_a_log_b.astype(F32)]),
        jnp.concatenate([zeros, dn_dt_bias_f.astype(F32), dn_dt_bias_b.astype(F32)])])
    gparams = jnp.pad(gparams, ((0, 0), (0, 128 - n_gate)))
    ut, za, qkv, zb, gates = _in_proj(x, mod3, norm_w, w_in, gparams, d_ssm, d_dn, n_gate, t)

    def mats(prm):
        lam_re, lam_im, log_step, b_re, b_im, c_re, c_im = prm
        pr, pi, br, bi = _s5_powers(lam_re, lam_im, log_step, b_re, b_im, t + 1)
        return pr, pi, br, bi, c_re.astype(F32), c_im.astype(F32)
    mi, bm, cm, lr, li = _s5_matrices(mats(s5f), mats(s5b), s5_d, t)
    yt = _s5_mix(ut.reshape(g, nc * bsz, t * p), mi, bm, cm, lr, li, nc, bsz)
    yt = yt.reshape(g, nc, bsz, t * p)

    o = _delta_net(qkv, conv_w.astype(F32), gates, nh)

    return _out_proj(x, mod3, yt, za, o, zb, glu_w, glu_b, dn_norm_w, w_out, out_norm_w, nh, t)


def kernel(x, c, ada_w, ada_b, norm_w, w_in, conv_w, dn_a_log_f, dn_dt_bias_f, dn_a_log_b,
           dn_dt_bias_b, dn_norm_w, lam_re_f, lam_im_f, log_step_f, b_re_f, b_im_f, c_re_f, c_im_f,
           lam_re_b, lam_im_b, log_step_b, b_re_b, b_im_b, c_re_b, c_im_b, s5_d, glu_w, glu_b,
           w_out, final_norm_w):
    assert ada_w.shape[0] == 1, "single-layer block"
    s5f = (lam_re_f[0], lam_im_f[0], log_step_f[0], b_re_f[0], b_im_f[0], c_re_f[0], c_im_f[0])
    s5b = (lam_re_b[0], lam_im_b[0], log_step_b[0], b_re_b[0], b_im_b[0], c_re_b[0], c_im_b[0])
    return _layer(x, c, ada_w[0], ada_b[0], norm_w[0], w_in[0], conv_w[0],
                  dn_a_log_f[0], dn_dt_bias_f[0], dn_a_log_b[0], dn_dt_bias_b[0], dn_norm_w[0],
                  s5f, s5b, s5_d[0], glu_w[0], glu_b[0], w_out[0], final_norm_w)
```

```python
import functools
import math

import jax
import jax.numpy as jnp
from jax import lax
from jax.experimental import pallas as pl
from jax.experimental.pallas import tpu as pltpu

F32 = jnp.float32
BF16 = jnp.bfloat16
HIGHEST = lax.Precision.HIGHEST

NORM_EPS = 1e-6
SSM_GROUP = 16
SSM_STATE = 64
DN_HEAD_DIM = 128
CONV_K = 5
S5_T = 16
DN_CHUNK = 64
DN_SUB = 16
VMEM_LIMIT = 56 * 1024 * 1024


def _cparams(sem):
    return pltpu.CompilerParams(dimension_semantics=sem, vmem_limit_bytes=VMEM_LIMIT)


def _bdot(a, b):
    return jnp.dot(a.astype(BF16), b.astype(BF16), preferred_element_type=F32)


def _bdot_nt(a, b):
    return lax.dot_general(a.astype(BF16), b.astype(BF16), (((1,), (1,)), ((), ())),
                           preferred_element_type=F32)


def _mod_body(c_ref, w_ref, b_ref, o_ref):
    c = c_ref[...]
    o_ref[...] = _bdot(c * jax.nn.sigmoid(c), w_ref[...]) + b_ref[...]


def _ada_mod(c, ada_w, ada_b):
    bsz, d = c.shape
    n = ada_w.shape[1]
    tn = 1024
    return pl.pallas_call(
        _mod_body,
        grid=(n // tn,),
        in_specs=[pl.BlockSpec((bsz, d), lambda j: (0, 0)),
                  pl.BlockSpec((d, tn), lambda j: (0, j)),
                  pl.BlockSpec((1, tn), lambda j: (0, j))],
        out_specs=pl.BlockSpec((bsz, tn), lambda j: (0, j)),
        out_shape=jax.ShapeDtypeStruct((bsz, n), F32),
        compiler_params=_cparams(("arbitrary",)),
        name="ada_mod",
    )(c, ada_w, ada_b.reshape(1, n))


def _lane_group(shape):
    return lax.broadcasted_iota(jnp.int32, shape, len(shape) - 1) // SSM_GROUP


def _inproj_body(x_ref, mod_ref, nw_ref, wu_ref, wza_ref, wq_ref, wk_ref, wv_ref, wzb_ref, wg_ref, gp_ref,
                 ut_ref, za_ref, qkv_ref, zb_ref, g_ref, slab_ref, *, t_len, dn_chunk):
    bsz, tt, d = x_ref.shape
    rows = bsz * tt
    x = x_ref[...]
    ms = jnp.mean(x * x, axis=-1, keepdims=True)
    y = x * lax.rsqrt(ms + NORM_EPS) * nw_ref[...]
    h = (y * (1.0 + mod_ref[:, 1:2, :]) + mod_ref[:, 0:1, :]).reshape(rows, d).astype(BF16)

    u = jnp.dot(h, wu_ref[...], preferred_element_type=F32)
    n_tile = u.shape[-1] // 128
    g_per_tile = 128 // SSM_GROUP
    for j in range(n_tile):
        slab_ref[j] = u[:, j * 128:(j + 1) * 128]
    grp = _lane_group((bsz, 128))
    for cc in range(tt // t_len):
        for j in range(n_tile):
            xs = [slab_ref[j, pl.ds(cc * t_len + s, bsz, stride=tt), :] for s in range(t_len)]
            for g8 in range(g_per_tile):
                for a in range(t_len // g_per_tile):
                    tile = jnp.zeros((bsz, 128), F32)
                    for sp in range(g_per_tile):
                        src = xs[a * g_per_tile + sp]
                        sh = (SSM_GROUP * (sp - g8)) % 128
                        tile = jnp.where(grp == sp, pltpu.roll(src, sh, 1) if sh else src, tile)
                    ut_ref[j * g_per_tile + g8, cc, :, a * 128:(a + 1) * 128] = tile.astype(BF16)

    r = jnp.dot(h, wg_ref[...], preferred_element_type=F32)
    n_gate = g_ref.shape[-2]
    lane = lax.broadcasted_iota(jnp.int32, r.shape, 1)
    gl = -jnp.exp(gp_ref[0:1, :]) * jax.nn.softplus(r + gp_ref[1:2, :])
    pos = lax.broadcasted_iota(jnp.int32, r.shape, 0) % dn_chunk
    pre = gl
    suf = gl
    sh = 1
    while sh < dn_chunk:
        pre = pre + jnp.where(pos >= sh, pltpu.roll(pre, sh, 0), 0.0)
        suf = suf + jnp.where(pos < dn_chunk - sh, pltpu.roll(suf, rows - sh, 0), 0.0)
        sh *= 2
    gv = jnp.where(lane < n_gate // 2, jax.nn.sigmoid(r), jnp.where(lane < 3 * n_gate // 4, pre, suf))
    for b in range(bsz):
        g_ref[0, b] = gv[b * tt:(b + 1) * tt].T[:n_gate]

    def proj(w_ref, out_ref, col=0):
        r = jnp.dot(h, w_ref[...], preferred_element_type=F32)
        out_ref[:, :, col:col + r.shape[-1]] = r.reshape(bsz, tt, r.shape[-1]).astype(out_ref.dtype)

    proj(wza_ref, za_ref)
    for k, w_ref in enumerate((wq_ref, wk_ref, wv_ref)):
        proj(w_ref, qkv_ref, k * w_ref.shape[-1])
    proj(wzb_ref, zb_ref)


def _in_proj(x, mod3, norm_w, w_in, gparams, d_ssm, d_dn, n_gate, t_len):
    bsz, seq, d = x.shape
    tt = DN_CHUNK
    o1 = d_ssm
    o2 = o1 + d_ssm
    o3 = o2 + 3 * d_dn
    o4 = o3 + d_dn
    wb = w_in.astype(BF16)
    wg = jnp.pad(wb[:, o4:], ((0, 0), (0, 128 - n_gate)))
    g = d_ssm // SSM_GROUP
    assert d_ssm == d_dn and o3 == o2 + 3 * d_dn
    wcol = lambda k: pl.BlockSpec((d, d_ssm), lambda i: (0, k))
    full = lambda a: pl.BlockSpec(a.shape, lambda i: (0,) * a.ndim)
    row = lambda w: pl.BlockSpec((bsz, tt, w), lambda i: (0, i, 0))
    return pl.pallas_call(
        functools.partial(_inproj_body, t_len=t_len, dn_chunk=DN_CHUNK),
        grid=(seq // tt,),
        in_specs=[row(d), full(mod3), pl.BlockSpec((1, d), lambda i: (0, 0)),
                  wcol(0), wcol(1), wcol(2), wcol(3), wcol(4), wcol(5), full(wg), full(gparams)],
        out_specs=[pl.BlockSpec((g, tt // t_len, bsz, t_len * SSM_GROUP), lambda i: (0, i, 0, 0)),
                   row(d_ssm), row(3 * d_dn), row(d_dn),
                   pl.BlockSpec((1, bsz, n_gate, tt), lambda i: (i, 0, 0, 0))],
        out_shape=[jax.ShapeDtypeStruct((g, seq // t_len, bsz, t_len * SSM_GROUP), BF16),
                   jax.ShapeDtypeStruct((bsz, seq, d_ssm), BF16),
                   jax.ShapeDtypeStruct((bsz, seq, 3 * d_dn), BF16),
                   jax.ShapeDtypeStruct((bsz, seq, d_dn), BF16),
                   jax.ShapeDtypeStruct((seq // tt, bsz, n_gate, tt), F32)],
        scratch_shapes=[pltpu.VMEM((d_ssm // 128, bsz * tt, 128), F32)],
        compiler_params=_cparams(("arbitrary",)),
        name="in_proj",
    )(x, mod3, norm_w.reshape(1, d), wb, wb, wb, wb, wb, wb, wg, gparams)


def _s5_operators(prm_f, prm_b, s5_d, t_len):
    stack = lambda i: jnp.stack([prm_f[i], prm_b[i]]).astype(F32)
    lr_, li_, log_step, b_re, b_im, c_re, c_im = (stack(i) for i in range(7))
    _, g, n, p = b_re.shape
    t = t_len
    dt = jnp.exp(log_step)[..., None]
    mag = jnp.exp(lr_ * dt)
    ab_re, ab_im = mag * jnp.cos(li_ * dt), mag * jnp.sin(li_ * dt)
    nr, ni = ab_re - 1.0, ab_im
    den = lr_ * lr_ + li_ * li_
    f_re = (nr * lr_ + ni * li_) / den
    f_im = (ni * lr_ - nr * li_) / den
    bb_re = f_re[..., None] * b_re - f_im[..., None] * b_im
    bb_im = f_re[..., None] * b_im + f_im[..., None] * b_re
    tau = jnp.arange(t + 1, dtype=F32)[:, None, None, None]
    pmag = jnp.exp(tau * (lr_ * dt)[None])
    ang = tau * (li_ * dt)[None]
    pr, pi = pmag * jnp.cos(ang), pmag * jnp.sin(ang)

    ein = functools.partial(jnp.einsum, precision=HIGHEST)
    xr = pr[:t, ..., None] * bb_re[None] - pi[:t, ..., None] * bb_im[None]
    xi = pr[:t, ..., None] * bb_im[None] + pi[:t, ..., None] * bb_re[None]
    k = ein("dgpn,tdgnq->tdgpq", c_re, xr) - ein("dgpn,tdgnq->tdgpq", c_im, xi)
    kf, kb = k[:, 0], k[:, 1]
    dmat = jnp.eye(p, dtype=F32)[None] * s5_d.astype(F32).reshape(g, p)[:, :, None]
    k0 = kf[0] + kb[0] + dmat
    kall = jnp.concatenate([kb[1:][::-1], k0[None], kf[1:]], axis=0).astype(BF16)
    idx = jnp.arange(t)[None, :] - jnp.arange(t)[:, None] + (t - 1)
    mi = kall[idx].transpose(2, 0, 4, 1, 3).reshape(g, t * p, t * p)

    s_idx = jnp.arange(t)
    pick = lambda a, of, ob: jnp.stack([a[of, 0], a[ob, 1]], axis=1)
    qr, qi = pick(pr, t - 1 - s_idx, s_idx), pick(pi, t - 1 - s_idx, s_idx)
    in_re = qr[..., None] * bb_re[None] - qi[..., None] * bb_im[None]
    in_im = qr[..., None] * bb_im[None] + qi[..., None] * bb_re[None]
    bm = jnp.stack([in_re, in_im]).astype(BF16)
    bm = bm.transpose(3, 1, 5, 0, 2, 4).reshape(g, t * p, 4 * n)
    qr, qi = pick(pr, s_idx + 1, t - s_idx), pick(pi, s_idx + 1, t - s_idx)
    w_re = c_re[None] * qr[..., None, :] - c_im[None] * qi[..., None, :]
    w_im = -(c_re[None] * qi[..., None, :] + c_im[None] * qr[..., None, :])
    cm = jnp.stack([w_re, w_im]).astype(BF16)
    cm = cm.transpose(3, 0, 2, 5, 1, 4).reshape(g, 4 * n, t * p)
    lr = jnp.concatenate([pr[t, 0], pr[t, 1]], axis=-1)[:, None, :]
    li = jnp.concatenate([pi[t, 0], pi[t, 1]], axis=-1)[:, None, :]
    return mi, bm, cm, lr, li


def _s5_body(u_ref, mi_ref, bm_ref, cm_ref, lr_ref, li_ref, y_ref, con_ref, s0_ref, s1_ref,
             *, gb, nc, bsz):
    n2 = lr_ref.shape[-1]
    for g in range(gb):
        con_ref[g] = jnp.dot(u_ref[g], bm_ref[g], preferred_element_type=F32)
    lane = lax.broadcasted_iota(jnp.int32, (bsz, 2 * n2), 1)
    is_fwd = (lane % n2) < (n2 // 2)
    lrs = [jnp.broadcast_to(lr_ref[g], (bsz, n2)) for g in range(gb)]
    lis = [jnp.broadcast_to(li_ref[g], (bsz, n2)) for g in range(gb)]

    def step(i, carry):
        rf = pl.multiple_of(i * bsz, bsz)
        rb = pl.multiple_of((nc - 1 - i) * bsz, bsz)
        out = []
        for g in range(gb):
            sre, sim = carry[2 * g], carry[2 * g + 1]
            st = jnp.concatenate([sre, sim], axis=-1).astype(BF16)
            s0_ref[g, pl.ds(rf, bsz), :] = st
            s1_ref[g, pl.ds(rb, bsz), :] = st
            c = jnp.where(is_fwd, con_ref[g, pl.ds(rf, bsz), :], con_ref[g, pl.ds(rb, bsz), :])
            nre = lrs[g] * sre - lis[g] * sim + c[:, :n2]
            nim = lrs[g] * sim + lis[g] * sre + c[:, n2:]
            out += [nre, nim]
        return tuple(out)

    zero = jnp.zeros((bsz, n2), F32)
    lax.fori_loop(0, nc, step, (zero,) * (2 * gb))

    rows = nc * bsz
    lane_all = lax.broadcasted_iota(jnp.int32, (rows, 2 * n2), 1)
    sel = (lane_all % n2) < (n2 // 2)
    for g in range(gb):
        xs = jnp.where(sel, s0_ref[g], s1_ref[g])
        y = (jnp.dot(u_ref[g], mi_ref[g], preferred_element_type=F32)
             + jnp.dot(xs, cm_ref[g], preferred_element_type=F32))
        y_ref[g] = jax.nn.gelu(y).astype(BF16)


def _s5_mix(ut, mi, bm, cm, lr, li, nc, bsz):
    g, rows, tp = ut.shape
    gb = 2
    n4 = bm.shape[-1]
    blk = lambda a: pl.BlockSpec((gb,) + a.shape[1:], lambda j: (j,) + (0,) * (a.ndim - 1))
    return pl.pallas_call(
        functools.partial(_s5_body, gb=gb, nc=nc, bsz=bsz),
        grid=(g // gb,),
        in_specs=[blk(ut), blk(mi), blk(bm), blk(cm), blk(lr), blk(li)],
        out_specs=blk(ut),
        out_shape=jax.ShapeDtypeStruct(ut.shape, BF16),
        scratch_shapes=[pltpu.VMEM((gb, rows, n4), F32),
                        pltpu.VMEM((gb, rows, n4), BF16),
                        pltpu.VMEM((gb, rows, n4), BF16)],
        compiler_params=_cparams(("arbitrary",)),
        name="s5_mix",
    )(ut, mi, bm, cm, lr, li)


def _dn_body(q_ref, k_ref, v_ref, cq_ref, ck_ref, cv_ref, gt_ref, o_ref,
             padq_ref, padk_ref, padv_ref, qs_ref, ks_ref, vs_ref, mq_ref, n_ref, el_ref, *, seq, ch, nh):
    hd = q_ref.shape[-1]
    h = pl.program_id(1)
    nch = seq // ch
    na_par = 8
    halo = 8
    rb = na_par * ch
    nrb = seq // rb

    srcs = ((q_ref, padq_ref, cq_ref, qs_ref, "q"), (k_ref, padk_ref, ck_ref, ks_ref, "k"),
            (v_ref, padv_ref, cv_ref, vs_ref, "v"))
    for _, pad_ref, _, _, _ in srcs:
        pad_ref[pl.ds(0, halo), :] = jnp.zeros((halo, hd), F32)
        pad_ref[pl.ds(halo + seq, halo), :] = jnp.zeros((halo, hd), F32)

    def fill(i, _):
        r0 = pl.multiple_of(i * rb, rb)
        for src_ref, pad_ref, _, _, _ in srcs:
            pad_ref[pl.ds(halo + r0, rb), :] = src_ref[0, pl.ds(r0, rb), :].astype(F32)
        return 0
    lax.fori_loop(0, nrb, fill, 0)

    def conv_rows(which, r0):
        _, pad_ref, w_ref, dst_ref, mode = srcs[which]
        acc = jnp.zeros((rb, hd), F32)
        for j in range(CONV_K):
            acc = acc + pad_ref[pl.ds(halo + r0 + (j - CONV_K // 2), rb), :] * w_ref[j:j + 1, :]
        a = acc * jax.nn.sigmoid(acc)
        if mode == "q":
            a = a * (lax.rsqrt(jnp.sum(a * a, axis=-1, keepdims=True) + NORM_EPS) * (hd ** -0.5))
        elif mode == "k":
            a = a * lax.rsqrt(jnp.sum(a * a, axis=-1, keepdims=True) + NORM_EPS)
        dst_ref[pl.ds(r0, rb), :] = a.astype(dst_ref.dtype)

    cols = (h, nh + h, 2 * nh + h, 3 * nh + h)

    ii = lax.broadcasted_iota(jnp.int32, (ch, ch), 0)
    jj = lax.broadcasted_iota(jnp.int32, (ch, ch), 1)

    def col_forms(row_a, row_b):
        assert 2 * ch == hd
        t = jnp.broadcast_to(jnp.concatenate([row_a, row_b], axis=-1), (hd, hd)).T
        return t[:ch], t[ch:]

    o_ref[...] = jnp.zeros(o_ref.shape, F32)
    masks = (ii >= jj, ii <= jj)
    stricts = (ii > jj, ii < jj)
    last_rows = (ch - 1, 0)
    eye = (ii == jj).astype(F32)
    same_blk = (ii // DN_SUB) == (jj // DN_SUB)
    n_sub_lvl = int(math.log2(DN_SUB))
    n_blk_lvl = int(math.log2(ch // DN_SUB))

    def prep_load(c, d):
        r0 = pl.multiple_of(c * ch, ch)
        return (d, qs_ref[pl.ds(r0, ch), :], ks_ref[pl.ds(r0, ch), :], vs_ref[pl.ds(r0, ch), :],
                gt_ref[c, 0, pl.ds(cols[d], 1), :], gt_ref[c, 0, pl.ds(cols[2 + d], 1), :])

    def prep_compute(loaded, tick):
        chains = []
        for v in loaded:
            both = _bdot_nt(jnp.concatenate([v[2], v[1]], axis=0), v[2])
            chains.append(dict(d=v[0], kk=both[:ch], qk=both[ch:]))
        tick()
        for cn, v in zip(chains, loaded):
            d, qc, kc, vc, br, gr = v
            beta, gc = col_forms(br, gr)
            kf, qf, vc = kc.astype(F32), qc.astype(F32), vc.astype(F32)
            mask = masks[d]
            decay = jnp.where(mask, jnp.exp(jnp.where(mask, gc[:, :ch] - gr, 0.0)), 0.0)
            a = jnp.where(stricts[d], beta[:, :ch] * cn["kk"] * decay, 0.0)
            a_d = jnp.where(same_blk, a, 0.0)
            egc = jnp.exp(gc)
            g_last = gc[last_rows[d]:last_rows[d] + 1, :]
            cn.update(a_d=a_d, dinv=eye - a_d,
                      rhs=jnp.concatenate([kf * (beta * egc), vc * beta, a - a_d], axis=-1),
                      attn=jnp.where(mask, cn["qk"] * decay, 0.0), qd=qf * egc,
                      kt=kf * jnp.exp(g_last - gc), el=jnp.exp(g_last))
        for cn in chains:
            cn["p"] = _bdot(cn["a_d"], cn["a_d"])
        tick()
        for lvl in range(1, n_sub_lvl):
            for cn in chains:
                if lvl + 1 < n_sub_lvl:
                    both = _bdot(jnp.concatenate([cn["dinv"], cn["p"]], axis=0), cn["p"])
                    cn["dinv"] = cn["dinv"] + both[:ch]
                    cn["p"] = both[ch:]
                else:
                    cn["dinv"] = cn["dinv"] + _bdot(cn["dinv"], cn["p"])
            tick()
        for cn in chains:
            first = _bdot(cn["dinv"], cn["rhs"])
            cn["sol"] = first[:, :2 * hd]
            cn["pw"] = first[:, 2 * hd:]
        tick()
        for lvl in range(n_blk_lvl):
            for cn in chains:
                upd = _bdot(cn["pw"], cn["sol"])
                cn["sol"] = cn["sol"] - upd if lvl == 0 else cn["sol"] + upd
                if lvl + 1 < n_blk_lvl:
                    cn["pw"] = _bdot(cn["pw"], cn["pw"])
            tick()
        for cn in chains:
            cn["kta"] = jnp.concatenate([cn["kt"].T, cn["attn"]], axis=0)
        tick()
        for cn in chains:
            cn["x"] = _bdot(cn["kta"], cn["sol"])
        tick()
        outs = []
        for cn in chains:
            x = cn["x"]
            outs.append((jnp.concatenate([-x[:hd, :hd], cn["qd"] - x[hd:, :hd]], axis=0).astype(BF16),
                         x[:hd, hd:], x[hd:, hd:], cn["el"]))
        return outs

    def prep_store(c, d, out):
        r0 = pl.multiple_of(c * ch, ch)
        mq_ref[d, c] = out[0]
        n_ref[d, c] = out[1]
        o_ref[0, pl.ds(r0, ch), :] += out[2]
        el_ref[d, pl.ds(c, 1), :] = out[3]

    n_iter = nch // na_par

    def chunk_of(i, j, d):
        return i * na_par + j if d == 0 else nch - 1 - (i * na_par + j)

    def prep_set(i, tick):
        todo = [(chunk_of(i, j, d), d) for j in range(na_par) for d in range(2)]
        outs = prep_compute([prep_load(c, d) for c, d in todo], tick)
        for (c, d), out in zip(todo, outs):
            prep_store(c, d, out)


    def state_step(i, j, states):
        new = []
        for d in range(2):
            c = chunk_of(i, j, d)
            r0 = pl.multiple_of(c * ch, ch)
            r = jnp.dot(mq_ref[d, c], states[d].astype(BF16), preferred_element_type=F32)
            o_ref[0, pl.ds(r0, ch), :] += r[hd:]
            new.append(states[d] * el_ref[d, pl.ds(c, 1), :] + r[:hd] + n_ref[d, c])
        return new

    first_blocks = sorted({0, nrb - 1})
    for blk in first_blocks:
        for which in range(len(srcs)):
            conv_rows(which, blk * rb)
    pending = [(which, blk * rb) for blk in range(nrb) if blk not in first_blocks
               for which in range(len(srcs))]

    def conv_tick(per_tick=2):
        for _ in range(min(per_tick, len(pending))):
            conv_rows(*pending.pop(0))
    prep_set(0, conv_tick)
    while pending:
        conv_tick()

    def pipelined(i, carry):
        box = dict(states=list(carry), j=0)

        def tick():
            if box["j"] < na_par:
                box["states"] = state_step(i - 1, box["j"], box["states"])
                box["j"] += 1
        prep_set(i, tick)
        while box["j"] < na_par:
            tick()
        return tuple(box["states"])

    z = jnp.zeros((hd, hd), F32)
    states = lax.fori_loop(1, n_iter, pipelined, (z, z))
    states = list(states)
    for j in range(na_par):
        states = state_step(n_iter - 1, j, states)


def _delta_net(qkv, conv_w, gates_t, nh):
    bsz, seq, _ = qkv.shape
    hd = DN_HEAD_DIM
    ch = DN_CHUNK
    nch = seq // ch
    ng = gates_t.shape[2]
    col = lambda off: pl.BlockSpec((1, seq, hd), lambda b, h: (b, 0, off + h))
    cw = lambda off: pl.BlockSpec((CONV_K, hd), lambda b, h: (0, off + h))
    return pl.pallas_call(
        functools.partial(_dn_body, seq=seq, ch=ch, nh=nh),
        grid=(bsz, nh),
        in_specs=[col(0), col(nh), col(2 * nh), cw(0), cw(nh), cw(2 * nh),
                  pl.BlockSpec((nch, 1, ng, ch), lambda b, h: (0, b, 0, 0))],
        out_specs=pl.BlockSpec((1, seq, hd), lambda b, h: (b, 0, h)),
        out_shape=jax.ShapeDtypeStruct((bsz, seq, nh * hd), F32),
        scratch_shapes=[pltpu.VMEM((seq + 16, hd), F32),
                        pltpu.VMEM((seq + 16, hd), F32),
                        pltpu.VMEM((seq + 16, hd), F32),
                        pltpu.VMEM((seq, hd), BF16),
                        pltpu.VMEM((seq, hd), BF16),
                        pltpu.VMEM((seq, hd), BF16),
                        pltpu.VMEM((2, nch, hd + ch, hd), BF16),
                        pltpu.VMEM((2, nch, hd, hd), F32),
                        pltpu.VMEM((2, nch, hd), F32)],
        compiler_params=_cparams(("arbitrary", "arbitrary")),
        name="delta_net",
    )(qkv, qkv, qkv, conv_w, conv_w, conv_w, gates_t)


def _out_body(x_ref, mod_ref, yt_ref, za_ref, o_ref, zb_ref, gw_ref, gbias_ref, dnw_ref,
              wo_ref, fw_ref, out_ref, slab_ref, *, nh, t_len):
    bsz, tt, d = x_ref.shape
    rows = bsz * tt
    d_ssm = za_ref.shape[-1]
    hd = o_ref.shape[-1] // nh
    n_tile = d_ssm // 128
    g_per_tile = 128 // SSM_GROUP

    mix = jnp.zeros((rows, d), F32)
    for hh in range(nh):
        o = o_ref[:, :, hh * hd:(hh + 1) * hd].reshape(rows, hd)
        zb = zb_ref[:, :, hh * hd:(hh + 1) * hd].reshape(rows, hd).astype(F32)
        o = o * lax.rsqrt(jnp.mean(o * o, axis=-1, keepdims=True) + NORM_EPS) * dnw_ref[...]
        yb = (o * (zb * jax.nn.sigmoid(zb))).astype(BF16)
        mix = mix + jnp.dot(yb, wo_ref[pl.ds(d_ssm + hh * hd, hd), :], preferred_element_type=F32)

    grp = _lane_group((bsz, 128))
    for cc in range(tt // t_len):
        for j in range(n_tile):
            for a in range(t_len // g_per_tile):
                ys = [yt_ref[j * g_per_tile + g8, cc, :, a * 128:(a + 1) * 128].astype(F32)
                      for g8 in range(g_per_tile)]
                for tp in range(g_per_tile):
                    tile = jnp.zeros((bsz, 128), F32)
                    for g8 in range(g_per_tile):
                        sh = (SSM_GROUP * (g8 - tp)) % 128
                        tile = jnp.where(grp == g8, pltpu.roll(ys[g8], sh, 1) if sh else ys[g8], tile)
                    t = a * g_per_tile + tp
                    slab_ref[j, pl.ds(cc * t_len + t, bsz, stride=tt), :] = tile
    ysn = jnp.concatenate([slab_ref[j] for j in range(n_tile)], axis=-1).astype(BF16)

    glu = jnp.dot(ysn, gw_ref[...], preferred_element_type=F32) + gbias_ref[...]
    za = za_ref[...].reshape(rows, d_ssm).astype(F32)
    ya = glu[:, :d_ssm] * jax.nn.sigmoid(glu[:, d_ssm:]) * (za * jax.nn.sigmoid(za))
    mix = mix + jnp.dot(ya.astype(BF16), wo_ref[pl.ds(0, d_ssm), :], preferred_element_type=F32)
    r = x_ref[...] + mod_ref[:, 2:3, :] * mix.reshape(bsz, tt, d)
    out_ref[...] = r * lax.rsqrt(jnp.mean(r * r, axis=-1, keepdims=True) + NORM_EPS) * fw_ref[...]


def _out_proj(x, mod3, yt, za, o, zb, glu_w, glu_b, dn_norm_w, w_out, final_norm_w, nh, t_len):
    bsz, seq, d = x.shape
    tt = 64
    g = yt.shape[0]
    full = lambda a: pl.BlockSpec(a.shape, lambda i: (0,) * a.ndim)
    row = lambda a: pl.BlockSpec((bsz, tt, a.shape[-1]), lambda i: (0, i, 0))
    gw = glu_w.astype(BF16)
    gbias = glu_b.reshape(1, -1)
    dnw = dn_norm_w.reshape(1, -1)
    wo = w_out.astype(BF16)
    fw = final_norm_w.reshape(1, -1)
    return pl.pallas_call(
        functools.partial(_out_body, nh=nh, t_len=t_len),
        grid=(seq // tt,),
        in_specs=[row(x), full(mod3),
                  pl.BlockSpec((g, tt // t_len, bsz, yt.shape[-1]), lambda i: (0, i, 0, 0)),
                  row(za), row(o), row(zb),
                  full(gw), full(gbias), full(dnw), full(wo), full(fw)],
        out_specs=row(x),
        out_shape=jax.ShapeDtypeStruct(x.shape, F32),
        scratch_shapes=[pltpu.VMEM((za.shape[-1] // 128, bsz * tt, 128), F32)],
        compiler_params=_cparams(("arbitrary",)),
        name="out_proj",
    )(x, mod3, yt, za, o, zb, gw, gbias, dnw, wo, fw)


def _layer(x, c, ada_w, ada_b, norm_w, w_in, conv_w,
           dn_a_log_f, dn_dt_bias_f, dn_a_log_b, dn_dt_bias_b, dn_norm_w,
           s5f, s5b, s5_d, glu_w, glu_b, w_out, out_norm_w):
    bsz, seq, d = x.shape
    d_ssm = s5_d.shape[0]
    d_dn = w_out.shape[0] - d_ssm
    nh = d_dn // DN_HEAD_DIM
    n_gate = 4 * nh
    p = SSM_GROUP
    g = d_ssm // p
    t = S5_T
    nc = seq // t

    mod3 = _ada_mod(c, ada_w, ada_b).reshape(bsz, 3, d)
    zeros = jnp.zeros((2 * nh,), F32)
    gparams = jnp.stack([
        jnp.concatenate([zeros, dn_a_log_f.astype(F32), dn_a_log_b.astype(F32)]),
        jnp.concatenate([zeros, dn_dt_bias_f.astype(F32), dn_dt_bias_b.astype(F32)])])
    gparams = jnp.pad(gparams, ((0, 0), (0, 128 - n_gate)))
    ut, za, qkv, zb, gates_t = _in_proj(x, mod3, norm_w, w_in, gparams, d_ssm, d_dn, n_gate, t)

    mi, bm, cm, lr, li = _s5_operators(s5f, s5b, s5_d, t)
    yt = _s5_mix(ut.reshape(g, nc * bsz, t * p), mi, bm, cm, lr, li, nc, bsz)
    yt = yt.reshape(g, nc, bsz, t * p)

    o = _delta_net(qkv, conv_w.astype(F32), gates_t, nh)

    return _out_proj(x, mod3, yt, za, o, zb, glu_w, glu_b, dn_norm_w, w_out, out_norm_w, nh, t)


def kernel(x, c, ada_w, ada_b, norm_w, w_in, conv_w, dn_a_log_f, dn_dt_bias_f, dn_a_log_b,
           dn_dt_bias_b, dn_norm_w, lam_re_f, lam_im_f, log_step_f, b_re_f, b_im_f, c_re_f, c_im_f,
           lam_re_b, lam_im_b, log_step_b, b_re_b, b_im_b, c_re_b, c_im_b, s5_d, glu_w, glu_b,
           w_out, final_norm_w):
    assert ada_w.shape[0] == 1, "single-layer block"
    s5f = (lam_re_f[0], lam_im_f[0], log_step_f[0], b_re_f[0], b_im_f[0], c_re_f[0], c_im_f[0])
    s5b = (lam_re_b[0], lam_im_b[0], log_step_b[0], b_re_b[0], b_im_b[0], c_re_b[0], c_im_b[0])
    return _layer(x, c, ada_w[0], ada_b[0], norm_w[0], w_in[0], conv_w[0],
                  dn_a_log_f[0], dn_dt_bias_f[0], dn_a_log_b[0], dn_dt_bias_b[0], dn_norm_w[0],
                  s5f, s5b, s5_d[0], glu_w[0], glu_b[0], w_out[0], final_norm_w)
```

```python
import functools
import math

import jax
import jax.numpy as jnp
from jax import lax
from jax.experimental import pallas as pl
from jax.experimental.pallas import tpu as pltpu

F32 = jnp.float32
BF16 = jnp.bfloat16
HIGHEST = lax.Precision.HIGHEST

NORM_EPS = 1e-6
SSM_GROUP = 16
LANES = 128
DN_HEAD_DIM = 128
CONV_K = 5
S5_T = 16
DN_CHUNK = 64
DN_SUB = 16
VMEM_LIMIT = 56 * 1024 * 1024


def _cparams(sem):
    return pltpu.CompilerParams(dimension_semantics=sem, vmem_limit_bytes=VMEM_LIMIT)


def _bdot(a, b):
    return jnp.dot(a.astype(BF16), b.astype(BF16), preferred_element_type=F32)


def _bdot_nt(a, b):
    return lax.dot_general(a.astype(BF16), b.astype(BF16), (((1,), (1,)), ((), ())),
                           preferred_element_type=F32)


def _mod_body(c_ref, w_ref, b_ref, o_ref):
    c = c_ref[...]
    o_ref[...] = _bdot(c * jax.nn.sigmoid(c), w_ref[...]) + b_ref[...]


def _ada_mod(c, ada_w, ada_b):
    bsz, d = c.shape
    n = ada_w.shape[1]
    tn = 1024
    return pl.pallas_call(
        _mod_body,
        grid=(n // tn,),
        in_specs=[pl.BlockSpec((bsz, d), lambda j: (0, 0)),
                  pl.BlockSpec((d, tn), lambda j: (0, j)),
                  pl.BlockSpec((1, tn), lambda j: (0, j))],
        out_specs=pl.BlockSpec((bsz, tn), lambda j: (0, j)),
        out_shape=jax.ShapeDtypeStruct((bsz, n), F32),
        compiler_params=_cparams(("arbitrary",)),
        name="ada_mod",
    )(c, ada_w, ada_b.reshape(1, n))


def _lane_group(shape):
    return lax.broadcasted_iota(jnp.int32, shape, len(shape) - 1) // SSM_GROUP


def _inproj_body(x_ref, mod_ref, nw_ref, wu_ref, wza_ref, wq_ref, wk_ref, wv_ref, wzb_ref, wg_ref, gp_ref,
                 ut_ref, za_ref, qkv_ref, zb_ref, g_ref, slab_ref, *, t_len, dn_chunk):
    bsz, tt, d = x_ref.shape
    rows = bsz * tt
    x = x_ref[...]
    ms = jnp.mean(x * x, axis=-1, keepdims=True)
    y = x * lax.rsqrt(ms + NORM_EPS) * nw_ref[...]
    h = (y * (1.0 + mod_ref[:, 1:2, :]) + mod_ref[:, 0:1, :]).reshape(rows, d).astype(BF16)

    u = jnp.dot(h, wu_ref[...], preferred_element_type=F32)
    n_tile = u.shape[-1] // LANES
    g_per_tile = LANES // SSM_GROUP
    for j in range(n_tile):
        slab_ref[j] = u[:, j * LANES:(j + 1) * LANES]
    grp = _lane_group((bsz, LANES))
    for cc in range(tt // t_len):
        for j in range(n_tile):
            xs = [slab_ref[j, pl.ds(cc * t_len + s, bsz, stride=tt), :] for s in range(t_len)]
            for g8 in range(g_per_tile):
                for a in range(t_len // g_per_tile):
                    tile = jnp.zeros((bsz, LANES), F32)
                    for sp in range(g_per_tile):
                        src = xs[a * g_per_tile + sp]
                        sh = (SSM_GROUP * (sp - g8)) % LANES
                        tile = jnp.where(grp == sp, pltpu.roll(src, sh, 1) if sh else src, tile)
                    ut_ref[j * g_per_tile + g8, cc, :, a * LANES:(a + 1) * LANES] = tile.astype(BF16)

    r = jnp.dot(h, wg_ref[...], preferred_element_type=F32)
    n_gate = g_ref.shape[-2]
    gt = jnp.concatenate([r[b * tt:(b + 1) * tt].T[:n_gate] for b in range(bsz)], axis=-1)
    row = lax.broadcasted_iota(jnp.int32, gt.shape, 0)
    gl = -jnp.exp(gp_ref[:, 0:1]) * jax.nn.softplus(gt + gp_ref[:, 1:2])
    pos = lax.broadcasted_iota(jnp.int32, gt.shape, 1) % dn_chunk
    pre = gl
    suf = gl
    sh = 1
    while sh < dn_chunk:
        pre = pre + jnp.where(pos >= sh, pltpu.roll(pre, sh, 1), 0.0)
        suf = suf + jnp.where(pos < dn_chunk - sh, pltpu.roll(suf, rows - sh, 1), 0.0)
        sh *= 2
    g_ref[0] = jnp.where(row < n_gate // 2, jax.nn.sigmoid(gt),
                         jnp.where(row < 3 * n_gate // 4, pre, suf))

    def proj(w_ref, out_ref, col=0):
        r = jnp.dot(h, w_ref[...], preferred_element_type=F32)
        out_ref[:, :, col:col + r.shape[-1]] = r.reshape(bsz, tt, r.shape[-1]).astype(out_ref.dtype)

    proj(wza_ref, za_ref)
    for k, w_ref in enumerate((wq_ref, wk_ref, wv_ref)):
        proj(w_ref, qkv_ref, k * w_ref.shape[-1])
    proj(wzb_ref, zb_ref)


def _in_proj(x, mod3, norm_w, w_in, gparams, d_ssm, d_dn, n_gate, t_len):
    bsz, seq, d = x.shape
    tt = DN_CHUNK
    o1 = d_ssm
    o2 = o1 + d_ssm
    o3 = o2 + 3 * d_dn
    o4 = o3 + d_dn
    wb = w_in.astype(BF16)
    wg = jnp.pad(wb[:, o4:], ((0, 0), (0, LANES - n_gate)))
    g = d_ssm // SSM_GROUP
    assert d_ssm == d_dn and o3 == o2 + 3 * d_dn
    wcol = lambda k: pl.BlockSpec((d, d_ssm), lambda i: (0, k))
    full = lambda a: pl.BlockSpec(a.shape, lambda i: (0,) * a.ndim)
    row = lambda w: pl.BlockSpec((bsz, tt, w), lambda i: (0, i, 0))
    return pl.pallas_call(
        functools.partial(_inproj_body, t_len=t_len, dn_chunk=DN_CHUNK),
        grid=(seq // tt,),
        in_specs=[row(d), full(mod3), pl.BlockSpec((1, d), lambda i: (0, 0)),
                  wcol(0), wcol(1), wcol(2), wcol(3), wcol(4), wcol(5), full(wg), full(gparams)],
        out_specs=[pl.BlockSpec((g, tt // t_len, bsz, t_len * SSM_GROUP), lambda i: (0, i, 0, 0)),
                   row(d_ssm), row(3 * d_dn), row(d_dn),
                   pl.BlockSpec((1, n_gate, bsz * tt), lambda i: (i, 0, 0))],
        out_shape=[jax.ShapeDtypeStruct((g, seq // t_len, bsz, t_len * SSM_GROUP), BF16),
                   jax.ShapeDtypeStruct((bsz, seq, d_ssm), BF16),
                   jax.ShapeDtypeStruct((bsz, seq, 3 * d_dn), BF16),
                   jax.ShapeDtypeStruct((bsz, seq, d_dn), BF16),
                   jax.ShapeDtypeStruct((seq // tt, n_gate, bsz * tt), F32)],
        scratch_shapes=[pltpu.VMEM((d_ssm // LANES, bsz * tt, LANES), F32)],
        compiler_params=_cparams(("arbitrary",)),
        name="in_proj",
    )(x, mod3, norm_w.reshape(1, d), wb, wb, wb, wb, wb, wb, wg, gparams)


def _s5_operators(prm_f, prm_b, s5_d, t_len):
    stack = lambda i: jnp.stack([prm_f[i], prm_b[i]]).astype(F32)
    lr_, li_, log_step, b_re, b_im, c_re, c_im = (stack(i) for i in range(7))
    _, g, n, p = b_re.shape
    t = t_len
    dt = jnp.exp(log_step)[..., None]
    mag = jnp.exp(lr_ * dt)
    ab_re, ab_im = mag * jnp.cos(li_ * dt), mag * jnp.sin(li_ * dt)
    nr, ni = ab_re - 1.0, ab_im
    den = lr_ * lr_ + li_ * li_
    f_re = (nr * lr_ + ni * li_) / den
    f_im = (ni * lr_ - nr * li_) / den
    bb_re = f_re[..., None] * b_re - f_im[..., None] * b_im
    bb_im = f_re[..., None] * b_im + f_im[..., None] * b_re
    tau = jnp.arange(t + 1, dtype=F32)[:, None, None, None]
    pmag = jnp.exp(tau * (lr_ * dt)[None])
    ang = tau * (li_ * dt)[None]
    pr, pi = pmag * jnp.cos(ang), pmag * jnp.sin(ang)

    ein = functools.partial(jnp.einsum, precision=HIGHEST)
    xr = pr[:t, ..., None] * bb_re[None] - pi[:t, ..., None] * bb_im[None]
    xi = pr[:t, ..., None] * bb_im[None] + pi[:t, ..., None] * bb_re[None]
    k = ein("dgpn,tdgnq->tdgpq", c_re, xr) - ein("dgpn,tdgnq->tdgpq", c_im, xi)
    kf, kb = k[:, 0], k[:, 1]
    dmat = jnp.eye(p, dtype=F32)[None] * s5_d.astype(F32).reshape(g, p)[:, :, None]
    k0 = kf[0] + kb[0] + dmat
    kall = jnp.concatenate([kb[1:][::-1], k0[None], kf[1:]], axis=0)
    kall = kall.astype(BF16).transpose(1, 3, 0, 2)
    mi = jnp.stack([kall[:, :, t - 1 - s:2 * t - 1 - s, :] for s in range(t)], axis=1)
    mi = mi.reshape(g, t * p, t * p)

    s_idx = jnp.arange(t)
    pick = lambda a, of, ob: jnp.stack([a[of, 0], a[ob, 1]], axis=1)
    qr = pick(pr, t - 1 - s_idx, s_idx).transpose(2, 0, 1, 3)[:, :, None, None]
    qi = pick(pi, t - 1 - s_idx, s_idx).transpose(2, 0, 1, 3)[:, :, None, None]
    br_, bi_ = (a.transpose(1, 3, 0, 2)[:, None, :, None] for a in (bb_re, bb_im))
    bm = jnp.concatenate([qr * br_ - qi * bi_, qr * bi_ + qi * br_], axis=3)
    bm = bm.astype(BF16).reshape(g, t * p, 4 * n)
    qr = pick(pr, s_idx + 1, t - s_idx).transpose(2, 1, 3, 0)[:, None, :, :, :, None]
    qi = pick(pi, s_idx + 1, t - s_idx).transpose(2, 1, 3, 0)[:, None, :, :, :, None]
    cr_, ci_ = (a.transpose(1, 0, 3, 2)[:, None, :, :, None, :] for a in (c_re, c_im))
    cm = jnp.concatenate([cr_ * qr - ci_ * qi, -(cr_ * qi + ci_ * qr)], axis=1)
    cm = cm.astype(BF16).reshape(g, 4 * n, t * p)
    lr = jnp.concatenate([pr[t, 0], pr[t, 1]], axis=-1)[:, None, :]
    li = jnp.concatenate([pi[t, 0], pi[t, 1]], axis=-1)[:, None, :]
    return mi, bm, cm, lr, li


def _s5_body(u_ref, mi_ref, bm_ref, cm_ref, lr_ref, li_ref, y_ref, con_ref, s0_ref, s1_ref,
             *, gb, nc, bsz):
    n2 = lr_ref.shape[-1]
    for g in range(gb):
        con_ref[g] = jnp.dot(u_ref[g], bm_ref[g], preferred_element_type=F32)
    lane = lax.broadcasted_iota(jnp.int32, (bsz, 2 * n2), 1)
    is_fwd = (lane % n2) < (n2 // 2)
    lrs = [jnp.broadcast_to(lr_ref[g], (bsz, n2)) for g in range(gb)]
    lis = [jnp.broadcast_to(li_ref[g], (bsz, n2)) for g in range(gb)]

    def step(i, carry):
        rf = pl.multiple_of(i * bsz, bsz)
        rb = pl.multiple_of((nc - 1 - i) * bsz, bsz)
        out = []
        for g in range(gb):
            sre, sim = carry[2 * g], carry[2 * g + 1]
            st = jnp.concatenate([sre, sim], axis=-1).astype(BF16)
            s0_ref[g, pl.ds(rf, bsz), :] = st
            s1_ref[g, pl.ds(rb, bsz), :] = st
            c = jnp.where(is_fwd, con_ref[g, pl.ds(rf, bsz), :], con_ref[g, pl.ds(rb, bsz), :])
            nre = lrs[g] * sre - lis[g] * sim + c[:, :n2]
            nim = lrs[g] * sim + lis[g] * sre + c[:, n2:]
            out += [nre, nim]
        return tuple(out)

    zero = jnp.zeros((bsz, n2), F32)
    lax.fori_loop(0, nc, step, (zero,) * (2 * gb))

    rows = nc * bsz
    lane_all = lax.broadcasted_iota(jnp.int32, (rows, 2 * n2), 1)
    sel = (lane_all % n2) < (n2 // 2)
    for g in range(gb):
        xs = jnp.where(sel, s0_ref[g], s1_ref[g])
        y = (jnp.dot(u_ref[g], mi_ref[g], preferred_element_type=F32)
             + jnp.dot(xs, cm_ref[g], preferred_element_type=F32))
        y_ref[g] = jax.nn.gelu(y).astype(BF16)


def _s5_mix(ut, mi, bm, cm, lr, li, nc, bsz):
    g, rows, tp = ut.shape
    gb = 2
    n4 = bm.shape[-1]
    blk = lambda a: pl.BlockSpec((gb,) + a.shape[1:], lambda j: (j,) + (0,) * (a.ndim - 1))
    return pl.pallas_call(
        functools.partial(_s5_body, gb=gb, nc=nc, bsz=bsz),
        grid=(g // gb,),
        in_specs=[blk(ut), blk(mi), blk(bm), blk(cm), blk(lr), blk(li)],
        out_specs=blk(ut),
        out_shape=jax.ShapeDtypeStruct(ut.shape, BF16),
        scratch_shapes=[pltpu.VMEM((gb, rows, n4), F32),
                        pltpu.VMEM((gb, rows, n4), BF16),
                        pltpu.VMEM((gb, rows, n4), BF16)],
        compiler_params=_cparams(("arbitrary",)),
        name="s5_mix",
    )(ut, mi, bm, cm, lr, li)


def _dn_body(q_ref, k_ref, v_ref, cq_ref, ck_ref, cv_ref, gt_ref, o_ref,
             padq_ref, padk_ref, padv_ref, qs_ref, ks_ref, vs_ref, wide_ref, mq_ref, n_ref, el_ref,
             *, seq, ch, nh):
    hd = q_ref.shape[-1]
    h = pl.program_id(1)
    nch = seq // ch
    na_par = 8
    halo = 8
    rb = na_par * ch
    nrb = seq // rb

    srcs = ((q_ref, padq_ref, cq_ref, qs_ref, "q"), (k_ref, padk_ref, ck_ref, ks_ref, "k"),
            (v_ref, padv_ref, cv_ref, vs_ref, "v"))
    for _, pad_ref, _, _, _ in srcs:
        pad_ref[pl.ds(0, halo), :] = jnp.zeros((halo, hd), F32)
        pad_ref[pl.ds(halo + seq, halo), :] = jnp.zeros((halo, hd), F32)

    def fill(i, _):
        r0 = pl.multiple_of(i * rb, rb)
        for src_ref, pad_ref, _, _, _ in srcs:
            pad_ref[pl.ds(halo + r0, rb), :] = src_ref[0, pl.ds(r0, rb), :].astype(F32)
        return 0
    lax.fori_loop(0, nrb, fill, 0)

    def conv_rows(which, r0):
        _, pad_ref, w_ref, dst_ref, mode = srcs[which]
        acc = jnp.zeros((rb, hd), F32)
        for j in range(CONV_K):
            acc = acc + pad_ref[pl.ds(halo + r0 + (j - CONV_K // 2), rb), :] * w_ref[j:j + 1, :]
        a = acc * jax.nn.sigmoid(acc)
        if mode == "q":
            a = a * (lax.rsqrt(jnp.sum(a * a, axis=-1, keepdims=True) + NORM_EPS) * (hd ** -0.5))
        elif mode == "k":
            a = a * lax.rsqrt(jnp.sum(a * a, axis=-1, keepdims=True) + NORM_EPS)
        dst_ref[pl.ds(r0, rb), :] = a.astype(dst_ref.dtype)

    cols = (h, nh + h, 2 * nh + h, 3 * nh + h)

    ii = lax.broadcasted_iota(jnp.int32, (ch, ch), 0)
    jj = lax.broadcasted_iota(jnp.int32, (ch, ch), 1)

    def col_forms(wide):
        assert 2 * ch == hd
        t = jnp.broadcast_to(wide, (hd, hd)).T
        return t[:ch], t[ch:]

    o_ref[...] = jnp.zeros(o_ref.shape, F32)
    masks = (ii >= jj, ii <= jj)
    stricts = (ii > jj, ii < jj)
    last_rows = (ch - 1, 0)
    eye = (ii == jj).astype(F32)
    same_blk = (ii // DN_SUB) == (jj // DN_SUB)
    n_sub_lvl = int(math.log2(DN_SUB))
    n_blk_lvl = int(math.log2(ch // DN_SUB))

    half = pl.program_id(0) % (hd // ch)
    gates_all = gt_ref[...]
    col_id = lax.broadcasted_iota(jnp.int32, gates_all.shape, 1)
    lane_id = lax.broadcasted_iota(jnp.int32, (nch, hd), 1)
    for d in range(2):
        rb_ = jnp.sum(jnp.where(col_id == cols[d], gates_all, 0.0), axis=1)
        rg_ = jnp.sum(jnp.where(col_id == cols[2 + d], gates_all, 0.0), axis=1)
        ours_b = jnp.where(half == 0, rb_, pltpu.roll(rb_, ch, 1))
        ours_g = jnp.where(half == 0, pltpu.roll(rg_, ch, 1), rg_)
        wide_ref[d] = jnp.where(lane_id < ch, ours_b, ours_g)

    def prep_load(c, d):
        r0 = pl.multiple_of(c * ch, ch)
        return (d, qs_ref[pl.ds(r0, ch), :], ks_ref[pl.ds(r0, ch), :], vs_ref[pl.ds(r0, ch), :],
                wide_ref[d, pl.ds(c, 1), :])

    def prep_compute(loaded, tick):
        chains = []
        for v in loaded:
            both = _bdot_nt(jnp.concatenate([v[2], v[1]], axis=0), v[2])
            chains.append(dict(d=v[0], kk=both[:ch], qk=both[ch:]))
        tick()
        for cn, v in zip(chains, loaded):
            d, qc, kc, vc, wide = v
            beta, gc = col_forms(wide)
            gr = wide[:, ch:]
            kf, qf, vc = kc.astype(F32), qc.astype(F32), vc.astype(F32)
            mask = masks[d]
            decay = jnp.where(mask, jnp.exp(jnp.where(mask, gc[:, :ch] - gr, 0.0)), 0.0)
            a = jnp.where(stricts[d], beta[:, :ch] * cn["kk"] * decay, 0.0)
            a_d = jnp.where(same_blk, a, 0.0)
            egc = jnp.exp(gc)
            g_last = gc[last_rows[d]:last_rows[d] + 1, :]
            cn.update(a_d=a_d, dinv=eye - a_d,
                      rhs=jnp.concatenate([kf * (beta * egc), vc * beta, a - a_d], axis=-1),
                      attn=jnp.where(mask, cn["qk"] * decay, 0.0), qd=qf * egc,
                      kt=kf * jnp.exp(g_last - gc), el=jnp.exp(g_last))
        for cn in chains:
            cn["p"] = _bdot(cn["a_d"], cn["a_d"])
        tick()
        for lvl in range(1, n_sub_lvl):
            for cn in chains:
                if lvl + 1 < n_sub_lvl:
                    both = _bdot(jnp.concatenate([cn["dinv"], cn["p"]], axis=0), cn["p"])
                    cn["dinv"] = cn["dinv"] + both[:ch]
                    cn["p"] = both[ch:]
                else:
                    cn["dinv"] = cn["dinv"] + _bdot(cn["dinv"], cn["p"])
            tick()
        for cn in chains:
            first = _bdot(cn["dinv"], cn["rhs"])
            cn["sol"] = first[:, :2 * hd]
            cn["pw"] = first[:, 2 * hd:]
        tick()
        for lvl in range(n_blk_lvl):
            for cn in chains:
                upd = _bdot(cn["pw"], cn["sol"])
                cn["sol"] = cn["sol"] - upd if lvl == 0 else cn["sol"] + upd
                if lvl + 1 < n_blk_lvl:
                    cn["pw"] = _bdot(cn["pw"], cn["pw"])
            tick()
        for cn in chains:
            cn["kta"] = jnp.concatenate([cn["kt"].T, cn["attn"]], axis=0)
        tick()
        for cn in chains:
            cn["x"] = _bdot(cn["kta"], cn["sol"])
        tick()
        outs = []
        for cn in chains:
            x = cn["x"]
            outs.append((jnp.concatenate([-x[:hd, :hd], cn["qd"] - x[hd:, :hd]], axis=0).astype(BF16),
                         x[:hd, hd:], x[hd:, hd:], cn["el"]))
        return outs

    def prep_store(c, d, out):
        r0 = pl.multiple_of(c * ch, ch)
        mq_ref[d, c] = out[0]
        n_ref[d, c] = out[1]
        o_ref[0, pl.ds(r0, ch), :] += out[2]
        el_ref[d, pl.ds(c, 1), :] = out[3]

    n_iter = nch // na_par

    def chunk_of(i, j, d):
        return i * na_par + j if d == 0 else nch - 1 - (i * na_par + j)

    def prep_set(i, tick):
        todo = [(chunk_of(i, j, d), d) for j in range(na_par) for d in range(2)]
        outs = prep_compute([prep_load(c, d) for c, d in todo], tick)
        for (c, d), out in zip(todo, outs):
            prep_store(c, d, out)


    def state_step(i, j, states):
        new = []
        for d in range(2):
            c = chunk_of(i, j, d)
            r0 = pl.multiple_of(c * ch, ch)
            r = jnp.dot(mq_ref[d, c], states[d].astype(BF16), preferred_element_type=F32)
            o_ref[0, pl.ds(r0, ch), :] += r[hd:]
            new.append(states[d] * el_ref[d, pl.ds(c, 1), :] + r[:hd] + n_ref[d, c])
        return new

    first_blocks = sorted({0, nrb - 1})
    for blk in first_blocks:
        for which in range(len(srcs)):
            conv_rows(which, blk * rb)
    pending = [(which, blk * rb) for blk in range(nrb) if blk not in first_blocks
               for which in range(len(srcs))]

    def conv_tick(per_tick=2):
        for _ in range(min(per_tick, len(pending))):
            conv_rows(*pending.pop(0))
    prep_set(0, conv_tick)
    while pending:
        conv_tick()

    def pipelined(i, carry):
        box = dict(states=list(carry), j=0)

        def tick():
            if box["j"] < na_par:
                box["states"] = state_step(i - 1, box["j"], box["states"])
                box["j"] += 1
        prep_set(i, tick)
        while box["j"] < na_par:
            tick()
        return tuple(box["states"])

    z = jnp.zeros((hd, hd), F32)
    states = lax.fori_loop(1, n_iter, pipelined, (z, z))
    states = list(states)
    for j in range(na_par):
        states = state_step(n_iter - 1, j, states)


def _delta_net(qkv, conv_w, gates_t, nh):
    bsz, seq, _ = qkv.shape
    hd = DN_HEAD_DIM
    ch = DN_CHUNK
    nch = seq // ch
    ng = gates_t.shape[1]
    per_blk = LANES // ch
    col = lambda off: pl.BlockSpec((1, seq, hd), lambda b, h: (b, 0, off + h))
    cw = lambda off: pl.BlockSpec((CONV_K, hd), lambda b, h: (0, off + h))
    return pl.pallas_call(
        functools.partial(_dn_body, seq=seq, ch=ch, nh=nh),
        grid=(bsz, nh),
        in_specs=[col(0), col(nh), col(2 * nh), cw(0), cw(nh), cw(2 * nh),
                  pl.BlockSpec((nch, ng, LANES), lambda b, h: (0, 0, b // per_blk))],
        out_specs=pl.BlockSpec((1, seq, hd), lambda b, h: (b, 0, h)),
        out_shape=jax.ShapeDtypeStruct((bsz, seq, nh * hd), F32),
        scratch_shapes=[pltpu.VMEM((seq + 16, hd), F32),
                        pltpu.VMEM((seq + 16, hd), F32),
                        pltpu.VMEM((seq + 16, hd), F32),
                        pltpu.VMEM((seq, hd), BF16),
                        pltpu.VMEM((seq, hd), BF16),
                        pltpu.VMEM((seq, hd), BF16),
                        pltpu.VMEM((2, nch, hd), F32),
                        pltpu.VMEM((2, nch, hd + ch, hd), BF16),
                        pltpu.VMEM((2, nch, hd, hd), F32),
                        pltpu.VMEM((2, nch, hd), F32)],
        compiler_params=_cparams(("arbitrary", "arbitrary")),
        name="delta_net",
    )(qkv, qkv, qkv, conv_w, conv_w, conv_w, gates_t)


def _out_body(x_ref, mod_ref, yt_ref, za_ref, o_ref, zb_ref, gw_ref, gbias_ref, dnw_ref,
              wo_ref, fw_ref, out_ref, slab_ref, *, nh, t_len):
    bsz, tt, d = x_ref.shape
    rows = bsz * tt
    d_ssm = za_ref.shape[-1]
    hd = o_ref.shape[-1] // nh
    n_tile = d_ssm // LANES
    g_per_tile = LANES // SSM_GROUP

    mix = jnp.zeros((rows, d), F32)
    for hh in range(nh):
        o = o_ref[:, :, hh * hd:(hh + 1) * hd].reshape(rows, hd)
        zb = zb_ref[:, :, hh * hd:(hh + 1) * hd].reshape(rows, hd).astype(F32)
        o = o * lax.rsqrt(jnp.mean(o * o, axis=-1, keepdims=True) + NORM_EPS) * dnw_ref[...]
        yb = (o * (zb * jax.nn.sigmoid(zb))).astype(BF16)
        mix = mix + jnp.dot(yb, wo_ref[pl.ds(d_ssm + hh * hd, hd), :], preferred_element_type=F32)

    grp = _lane_group((bsz, LANES))
    for cc in range(tt // t_len):
        for j in range(n_tile):
            for a in range(t_len // g_per_tile):
                ys = [yt_ref[j * g_per_tile + g8, cc, :, a * LANES:(a + 1) * LANES].astype(F32)
                      for g8 in range(g_per_tile)]
                for tp in range(g_per_tile):
                    tile = jnp.zeros((bsz, LANES), F32)
                    for g8 in range(g_per_tile):
                        sh = (SSM_GROUP * (g8 - tp)) % LANES
                        tile = jnp.where(grp == g8, pltpu.roll(ys[g8], sh, 1) if sh else ys[g8], tile)
                    t = a * g_per_tile + tp
                    slab_ref[j, pl.ds(cc * t_len + t, bsz, stride=tt), :] = tile
    ysn = jnp.concatenate([slab_ref[j] for j in range(n_tile)], axis=-1).astype(BF16)

    glu = jnp.dot(ysn, gw_ref[...], preferred_element_type=F32) + gbias_ref[...]
    za = za_ref[...].reshape(rows, d_ssm).astype(F32)
    ya = glu[:, :d_ssm] * jax.nn.sigmoid(glu[:, d_ssm:]) * (za * jax.nn.sigmoid(za))
    mix = mix + jnp.dot(ya.astype(BF16), wo_ref[pl.ds(0, d_ssm), :], preferred_element_type=F32)
    r = x_ref[...] + mod_ref[:, 2:3, :] * mix.reshape(bsz, tt, d)
    out_ref[...] = r * lax.rsqrt(jnp.mean(r * r, axis=-1, keepdims=True) + NORM_EPS) * fw_ref[...]


def _out_proj(x, mod3, yt, za, o, zb, glu_w, glu_b, dn_norm_w, w_out, final_norm_w, nh, t_len):
    bsz, seq, d = x.shape
    tt = 64
    g = yt.shape[0]
    full = lambda a: pl.BlockSpec(a.shape, lambda i: (0,) * a.ndim)
    row = lambda a: pl.BlockSpec((bsz, tt, a.shape[-1]), lambda i: (0, i, 0))
    gw = glu_w.astype(BF16)
    gbias = glu_b.reshape(1, -1)
    dnw = dn_norm_w.reshape(1, -1)
    wo = w_out.astype(BF16)
    fw = final_norm_w.reshape(1, -1)
    return pl.pallas_call(
        functools.partial(_out_body, nh=nh, t_len=t_len),
        grid=(seq // tt,),
        in_specs=[row(x), full(mod3),
                  pl.BlockSpec((g, tt // t_len, bsz, yt.shape[-1]), lambda i: (0, i, 0, 0)),
                  row(za), row(o), row(zb),
                  full(gw), full(gbias), full(dnw), full(wo), full(fw)],
        out_specs=row(x),
        out_shape=jax.ShapeDtypeStruct(x.shape, F32),
        scratch_shapes=[pltpu.VMEM((za.shape[-1] // LANES, bsz * tt, LANES), F32)],
        compiler_params=_cparams(("arbitrary",)),
        name="out_proj",
    )(x, mod3, yt, za, o, zb, gw, gbias, dnw, wo, fw)


def _layer(x, c, ada_w, ada_b, norm_w, w_in, conv_w,
           dn_a_log_f, dn_dt_bias_f, dn_a_log_b, dn_dt_bias_b, dn_norm_w,
           s5f, s5b, s5_d, glu_w, glu_b, w_out, out_norm_w):
    bsz, seq, d = x.shape
    d_ssm = s5_d.shape[0]
    d_dn = w_out.shape[0] - d_ssm
    nh = d_dn // DN_HEAD_DIM
    n_gate = 4 * nh
    p = SSM_GROUP
    g = d_ssm // p
    t = S5_T
    nc = seq // t

    mod3 = _ada_mod(c, ada_w, ada_b).reshape(bsz, 3, d)
    zeros = jnp.zeros((2 * nh,), F32)
    gparams = jnp.stack([
        jnp.concatenate([zeros, dn_a_log_f.astype(F32), dn_a_log_b.astype(F32)]),
        jnp.concatenate([zeros, dn_dt_bias_f.astype(F32), dn_dt_bias_b.astype(F32)])])
    gparams = jnp.pad(gparams.T, ((0, 0), (0, LANES - 2)))
    ut, za, qkv, zb, gates_t = _in_proj(x, mod3, norm_w, w_in, gparams, d_ssm, d_dn, n_gate, t)

    mi, bm, cm, lr, li = _s5_operators(s5f, s5b, s5_d, t)
    yt = _s5_mix(ut.reshape(g, nc * bsz, t * p), mi, bm, cm, lr, li, nc, bsz)
    yt = yt.reshape(g, nc, bsz, t * p)

    o = _delta_net(qkv, conv_w.astype(F32), gates_t, nh)

    return _out_proj(x, mod3, yt, za, o, zb, glu_w, glu_b, dn_norm_w, w_out, out_norm_w, nh, t)


def kernel(x, c, ada_w, ada_b, norm_w, w_in, conv_w, dn_a_log_f, dn_dt_bias_f, dn_a_log_b,
           dn_dt_bias_b, dn_norm_w, lam_re_f, lam_im_f, log_step_f, b_re_f, b_im_f, c_re_f, c_im_f,
           lam_re_b, lam_im_b, log_step_b, b_re_b, b_im_b, c_re_b, c_im_b, s5_d, glu_w, glu_b,
           w_out, final_norm_w):
    assert ada_w.shape[0] == 1, "single-layer block"
    s5f = (lam_re_f[0], lam_im_f[0], log_step_f[0], b_re_f[0], b_im_f[0], c_re_f[0], c_im_f[0])
    s5b = (lam_re_b[0], lam_im_b[0], log_step_b[0], b_re_b[0], b_im_b[0], c_re_b[0], c_im_b[0])
    return _layer(x, c, ada_w[0], ada_b[0], norm_w[0], w_in[0], conv_w[0],
                  dn_a_log_f[0], dn_dt_bias_f[0], dn_a_log_b[0], dn_dt_bias_b[0], dn_norm_w[0],
                  s5f, s5b, s5_d[0], glu_w[0], glu_b[0], w_out[0], final_norm_w)
```

```python
import functools
import math

import jax
import jax.numpy as jnp
from jax import lax
from jax.experimental import pallas as pl
from jax.experimental.pallas import tpu as pltpu

F32 = jnp.float32
BF16 = jnp.bfloat16
HIGHEST = lax.Precision.HIGHEST

NORM_EPS = 1e-6
SSM_GROUP = 16
LANES = 128
DN_HEAD_DIM = 128
CONV_K = 5
S5_T = 16
DN_CHUNK = 64
DN_SUB = 16
VMEM_LIMIT = 56 * 1024 * 1024


def _cparams(sem):
    return pltpu.CompilerParams(dimension_semantics=sem, vmem_limit_bytes=VMEM_LIMIT)


def _bdot(a, b):
    return jnp.dot(a.astype(BF16), b.astype(BF16), preferred_element_type=F32)


def _bdot_nt(a, b):
    return lax.dot_general(a.astype(BF16), b.astype(BF16), (((1,), (1,)), ((), ())),
                           preferred_element_type=F32)


def _mod_body(c_ref, w_ref, b_ref, o_ref):
    c = c_ref[...]
    o_ref[...] = _bdot(c * jax.nn.sigmoid(c), w_ref[...]) + b_ref[...]


def _ada_mod(c, ada_w, ada_b):
    bsz, d = c.shape
    n = ada_w.shape[1]
    tn = 1024
    return pl.pallas_call(
        _mod_body,
        grid=(n // tn,),
        in_specs=[pl.BlockSpec((bsz, d), lambda j: (0, 0)),
                  pl.BlockSpec((d, tn), lambda j: (0, j)),
                  pl.BlockSpec((1, tn), lambda j: (0, j))],
        out_specs=pl.BlockSpec((bsz, tn), lambda j: (0, j)),
        out_shape=jax.ShapeDtypeStruct((bsz, n), F32),
        compiler_params=_cparams(("arbitrary",)),
        name="ada_mod",
    )(c, ada_w, ada_b.reshape(1, n))


def _lane_group(shape):
    return lax.broadcasted_iota(jnp.int32, shape, len(shape) - 1) // SSM_GROUP


def _inproj_body(x_ref, mod_ref, nw_ref, wu_ref, wza_ref, wq_ref, wk_ref, wv_ref, wzb_ref, wg_ref, gp_ref,
                 ut_ref, za_ref, qkv_ref, zb_ref, g_ref, slab_ref, *, t_len, dn_chunk):
    bsz, tt, d = x_ref.shape
    rows = bsz * tt
    x = x_ref[...]
    ms = jnp.mean(x * x, axis=-1, keepdims=True)
    y = x * lax.rsqrt(ms + NORM_EPS) * nw_ref[...]
    h = (y * (1.0 + mod_ref[:, 1:2, :]) + mod_ref[:, 0:1, :]).reshape(rows, d).astype(BF16)

    u = jnp.dot(h, wu_ref[...], preferred_element_type=F32)
    n_tile = u.shape[-1] // LANES
    g_per_tile = LANES // SSM_GROUP
    for j in range(n_tile):
        slab_ref[j] = u[:, j * LANES:(j + 1) * LANES]
    grp = _lane_group((bsz, LANES))
    for cc in range(tt // t_len):
        for j in range(n_tile):
            xs = [slab_ref[j, pl.ds(cc * t_len + s, bsz, stride=tt), :] for s in range(t_len)]
            for g8 in range(g_per_tile):
                for a in range(t_len // g_per_tile):
                    tile = jnp.zeros((bsz, LANES), F32)
                    for sp in range(g_per_tile):
                        src = xs[a * g_per_tile + sp]
                        sh = (SSM_GROUP * (sp - g8)) % LANES
                        tile = jnp.where(grp == sp, pltpu.roll(src, sh, 1) if sh else src, tile)
                    ut_ref[j * g_per_tile + g8, cc, :, a * LANES:(a + 1) * LANES] = tile.astype(BF16)

    r = jnp.dot(h, wg_ref[...], preferred_element_type=F32)
    n_gate = g_ref.shape[-2]
    gt = jnp.concatenate([r[b * tt:(b + 1) * tt].T[:n_gate] for b in range(bsz)], axis=-1)
    row = lax.broadcasted_iota(jnp.int32, gt.shape, 0)
    gl = -jnp.exp(gp_ref[:, 0:1]) * jax.nn.softplus(gt + gp_ref[:, 1:2])
    pos = lax.broadcasted_iota(jnp.int32, gt.shape, 1) % dn_chunk
    pre = gl
    suf = gl
    sh = 1
    while sh < dn_chunk:
        pre = pre + jnp.where(pos >= sh, pltpu.roll(pre, sh, 1), 0.0)
        suf = suf + jnp.where(pos < dn_chunk - sh, pltpu.roll(suf, rows - sh, 1), 0.0)
        sh *= 2
    g_ref[0] = jnp.where(row < n_gate // 2, jax.nn.sigmoid(gt),
                         jnp.where(row < 3 * n_gate // 4, pre, suf))

    def proj(w_ref, out_ref, col=0):
        r = jnp.dot(h, w_ref[...], preferred_element_type=F32)
        out_ref[:, :, col:col + r.shape[-1]] = r.reshape(bsz, tt, r.shape[-1]).astype(out_ref.dtype)

    proj(wza_ref, za_ref)
    for k, w_ref in enumerate((wq_ref, wk_ref, wv_ref)):
        proj(w_ref, qkv_ref, k * w_ref.shape[-1])
    proj(wzb_ref, zb_ref)


def _in_proj(x, mod3, norm_w, w_in, gparams, d_ssm, d_dn, n_gate, t_len):
    bsz, seq, d = x.shape
    tt = DN_CHUNK
    o1 = d_ssm
    o2 = o1 + d_ssm
    o3 = o2 + 3 * d_dn
    o4 = o3 + d_dn
    wb = w_in.astype(BF16)
    wg = jnp.pad(wb[:, o4:], ((0, 0), (0, LANES - n_gate)))
    g = d_ssm // SSM_GROUP
    assert d_ssm == d_dn and o3 == o2 + 3 * d_dn
    wcol = lambda k: pl.BlockSpec((d, d_ssm), lambda i: (0, k))
    full = lambda a: pl.BlockSpec(a.shape, lambda i: (0,) * a.ndim)
    row = lambda w: pl.BlockSpec((bsz, tt, w), lambda i: (0, i, 0))
    return pl.pallas_call(
        functools.partial(_inproj_body, t_len=t_len, dn_chunk=DN_CHUNK),
        grid=(seq // tt,),
        in_specs=[row(d), full(mod3), pl.BlockSpec((1, d), lambda i: (0, 0)),
                  wcol(0), wcol(1), wcol(2), wcol(3), wcol(4), wcol(5), full(wg), full(gparams)],
        out_specs=[pl.BlockSpec((g, tt // t_len, bsz, t_len * SSM_GROUP), lambda i: (0, i, 0, 0)),
                   row(d_ssm), row(3 * d_dn), row(d_dn),
                   pl.BlockSpec((1, n_gate, bsz * tt), lambda i: (i, 0, 0))],
        out_shape=[jax.ShapeDtypeStruct((g, seq // t_len, bsz, t_len * SSM_GROUP), BF16),
                   jax.ShapeDtypeStruct((bsz, seq, d_ssm), BF16),
                   jax.ShapeDtypeStruct((bsz, seq, 3 * d_dn), BF16),
                   jax.ShapeDtypeStruct((bsz, seq, d_dn), BF16),
                   jax.ShapeDtypeStruct((seq // tt, n_gate, bsz * tt), F32)],
        scratch_shapes=[pltpu.VMEM((d_ssm // LANES, bsz * tt, LANES), F32)],
        compiler_params=_cparams(("arbitrary",)),
        name="in_proj",
    )(x, mod3, norm_w.reshape(1, d), wb, wb, wb, wb, wb, wb, wg, gparams)


def _s5_operators(prm_f, prm_b, s5_d, t_len):
    stack = lambda i: jnp.stack([prm_f[i], prm_b[i]]).astype(F32)
    lr_, li_, log_step, b_re, b_im, c_re, c_im = (stack(i) for i in range(7))
    _, g, n, p = b_re.shape
    t = t_len
    dt = jnp.exp(log_step)[..., None]
    mag = jnp.exp(lr_ * dt)
    ab_re, ab_im = mag * jnp.cos(li_ * dt), mag * jnp.sin(li_ * dt)
    nr, ni = ab_re - 1.0, ab_im
    den = lr_ * lr_ + li_ * li_
    f_re = (nr * lr_ + ni * li_) / den
    f_im = (ni * lr_ - nr * li_) / den
    bb_re = f_re[..., None] * b_re - f_im[..., None] * b_im
    bb_im = f_re[..., None] * b_im + f_im[..., None] * b_re
    tau = jnp.arange(t + 1, dtype=F32)[:, None, None, None]
    pmag = jnp.exp(tau * (lr_ * dt)[None])
    ang = tau * (li_ * dt)[None]
    pr, pi = pmag * jnp.cos(ang), pmag * jnp.sin(ang)

    ein = functools.partial(jnp.einsum, precision=HIGHEST)
    xr = pr[:t, ..., None] * bb_re[None] - pi[:t, ..., None] * bb_im[None]
    xi = pr[:t, ..., None] * bb_im[None] + pi[:t, ..., None] * bb_re[None]
    k = ein("dgpn,tdgnq->tdgpq", c_re, xr) - ein("dgpn,tdgnq->tdgpq", c_im, xi)
    kf, kb = k[:, 0], k[:, 1]
    dmat = jnp.eye(p, dtype=F32)[None] * s5_d.astype(F32).reshape(g, p)[:, :, None]
    k0 = kf[0] + kb[0] + dmat
    kall = jnp.concatenate([kb[1:][::-1], k0[None], kf[1:]], axis=0)
    kall = kall.astype(BF16).transpose(1, 3, 0, 2).reshape(g, p, (2 * t - 1) * p)
    src = lax.broadcasted_iota(jnp.int32, (t, (2 * t - 1) * p, t * p), 1)
    dst = lax.broadcasted_iota(jnp.int32, (t, (2 * t - 1) * p, t * p), 2)
    shift = (t - 1 - lax.broadcasted_iota(jnp.int32, (t, (2 * t - 1) * p, t * p), 0)) * p
    sel = (src == dst + shift).astype(BF16)
    mi = jnp.einsum("gql,slk->gsqk", kall, sel, preferred_element_type=F32).astype(BF16)
    mi = mi.reshape(g, t * p, t * p)

    s_idx = jnp.arange(t)
    pick = lambda a, of, ob: jnp.stack([a[of, 0], a[ob, 1]], axis=1)
    dn = lambda a: a.transpose(2, 0, 1, 3).reshape(g, -1, 2 * n)
    qr, qi = dn(pick(pr, t - 1 - s_idx, s_idx)), dn(pick(pi, t - 1 - s_idx, s_idx))
    br_, bi_ = (dn(a.transpose(3, 0, 1, 2)) for a in (bb_re, bb_im))
    cat = lambda a, b_: jnp.concatenate([a, b_], axis=-1)[:, :, None, :]
    bm = (cat(qr, qr) * jnp.concatenate([br_, bi_], axis=-1)[:, None]
          + cat(qi, qi) * jnp.concatenate([-bi_, br_], axis=-1)[:, None])
    bm = bm.astype(BF16).reshape(g, t * p, 4 * n)
    rows_of = lambda a: a.transpose(2, 1, 3, 0).reshape(g, 2 * n, -1)
    qr = jnp.repeat(rows_of(pick(pr, s_idx + 1, t - s_idx)), p, axis=-1)
    qi = jnp.repeat(rows_of(pick(pi, s_idx + 1, t - s_idx)), p, axis=-1)
    cr_, ci_ = (jnp.tile(rows_of(a.transpose(2, 0, 1, 3)), (1, 1, t)) for a in (c_re, c_im))
    cm = jnp.concatenate([cr_ * qr - ci_ * qi, -(cr_ * qi + ci_ * qr)], axis=1).astype(BF16)
    lr = jnp.concatenate([pr[t, 0], pr[t, 1]], axis=-1)[:, None, :]
    li = jnp.concatenate([pi[t, 0], pi[t, 1]], axis=-1)[:, None, :]
    return mi, bm, cm, lr, li


def _s5_body(u_ref, mi_ref, bm_ref, cm_ref, lr_ref, li_ref, y_ref, con_ref, s0_ref, s1_ref,
             *, gb, nc, bsz):
    n2 = lr_ref.shape[-1]
    for g in range(gb):
        con_ref[g] = jnp.dot(u_ref[g], bm_ref[g], preferred_element_type=F32)
    lane = lax.broadcasted_iota(jnp.int32, (bsz, 2 * n2), 1)
    is_fwd = (lane % n2) < (n2 // 2)
    lrs = [jnp.broadcast_to(lr_ref[g], (bsz, n2)) for g in range(gb)]
    lis = [jnp.broadcast_to(li_ref[g], (bsz, n2)) for g in range(gb)]

    def step(i, carry):
        rf = pl.multiple_of(i * bsz, bsz)
        rb = pl.multiple_of((nc - 1 - i) * bsz, bsz)
        out = []
        for g in range(gb):
            sre, sim = carry[2 * g], carry[2 * g + 1]
            st = jnp.concatenate([sre, sim], axis=-1).astype(BF16)
            s0_ref[g, pl.ds(rf, bsz), :] = st
            s1_ref[g, pl.ds(rb, bsz), :] = st
            c = jnp.where(is_fwd, con_ref[g, pl.ds(rf, bsz), :], con_ref[g, pl.ds(rb, bsz), :])
            nre = lrs[g] * sre - lis[g] * sim + c[:, :n2]
            nim = lrs[g] * sim + lis[g] * sre + c[:, n2:]
            out += [nre, nim]
        return tuple(out)

    zero = jnp.zeros((bsz, n2), F32)
    lax.fori_loop(0, nc, step, (zero,) * (2 * gb))

    rows = nc * bsz
    lane_all = lax.broadcasted_iota(jnp.int32, (rows, 2 * n2), 1)
    sel = (lane_all % n2) < (n2 // 2)
    for g in range(gb):
        xs = jnp.where(sel, s0_ref[g], s1_ref[g])
        y = (jnp.dot(u_ref[g], mi_ref[g], preferred_element_type=F32)
             + jnp.dot(xs, cm_ref[g], preferred_element_type=F32))
        y_ref[g] = jax.nn.gelu(y).astype(BF16)


def _s5_mix(ut, mi, bm, cm, lr, li, nc, bsz):
    g, rows, tp = ut.shape
    gb = 2
    n4 = bm.shape[-1]
    blk = lambda a: pl.BlockSpec((gb,) + a.shape[1:], lambda j: (j,) + (0,) * (a.ndim - 1))
    return pl.pallas_call(
        functools.partial(_s5_body, gb=gb, nc=nc, bsz=bsz),
        grid=(g // gb,),
        in_specs=[blk(ut), blk(mi), blk(bm), blk(cm), blk(lr), blk(li)],
        out_specs=blk(ut),
        out_shape=jax.ShapeDtypeStruct(ut.shape, BF16),
        scratch_shapes=[pltpu.VMEM((gb, rows, n4), F32),
                        pltpu.VMEM((gb, rows, n4), BF16),
                        pltpu.VMEM((gb, rows, n4), BF16)],
        compiler_params=_cparams(("arbitrary",)),
        name="s5_mix",
    )(ut, mi, bm, cm, lr, li)


def _dn_body(q_ref, k_ref, v_ref, cq_ref, ck_ref, cv_ref, gt_ref, o_ref,
             padq_ref, padk_ref, padv_ref, qs_ref, ks_ref, vs_ref, wide_ref, mq_ref, n_ref, el_ref,
             *, seq, ch, nh):
    hd = q_ref.shape[-1]
    h = pl.program_id(1)
    nch = seq // ch
    na_par = 8
    halo = 8
    rb = 256
    nrb = seq // rb
    set_blocks = na_par * ch // rb

    srcs = ((q_ref, padq_ref, cq_ref, qs_ref, "q"), (k_ref, padk_ref, ck_ref, ks_ref, "k"),
            (v_ref, padv_ref, cv_ref, vs_ref, "v"))
    for _, pad_ref, _, _, _ in srcs:
        pad_ref[pl.ds(0, halo), :] = jnp.zeros((halo, hd), F32)
        pad_ref[pl.ds(halo + seq, halo), :] = jnp.zeros((halo, hd), F32)

    def fill(i, _):
        r0 = pl.multiple_of(i * rb, rb)
        for src_ref, pad_ref, _, _, _ in srcs:
            pad_ref[pl.ds(halo + r0, rb), :] = src_ref[0, pl.ds(r0, rb), :].astype(F32)
        return 0
    lax.fori_loop(0, nrb, fill, 0)

    def conv_rows(which, r0):
        _, pad_ref, w_ref, dst_ref, mode = srcs[which]
        acc = jnp.zeros((rb, hd), F32)
        for j in range(CONV_K):
            acc = acc + pad_ref[pl.ds(halo + r0 + (j - CONV_K // 2), rb), :] * w_ref[j:j + 1, :]
        a = acc * jax.nn.sigmoid(acc)
        if mode == "q":
            a = a * (lax.rsqrt(jnp.sum(a * a, axis=-1, keepdims=True) + NORM_EPS) * (hd ** -0.5))
        elif mode == "k":
            a = a * lax.rsqrt(jnp.sum(a * a, axis=-1, keepdims=True) + NORM_EPS)
        dst_ref[pl.ds(r0, rb), :] = a.astype(dst_ref.dtype)

    cols = (h, nh + h, 2 * nh + h, 3 * nh + h)

    ii = lax.broadcasted_iota(jnp.int32, (ch, ch), 0)
    jj = lax.broadcasted_iota(jnp.int32, (ch, ch), 1)

    def col_forms(wide):
        assert 2 * ch == hd
        t = jnp.broadcast_to(wide, (hd, hd)).T
        return t[:ch], t[ch:]

    o_ref[...] = jnp.zeros(o_ref.shape, F32)
    masks = (ii >= jj, ii <= jj)
    stricts = (ii > jj, ii < jj)
    last_rows = (ch - 1, 0)
    eye = (ii == jj).astype(F32)
    same_blk = (ii // DN_SUB) == (jj // DN_SUB)
    n_sub_lvl = int(math.log2(DN_SUB))
    n_blk_lvl = int(math.log2(ch // DN_SUB))

    half = pl.program_id(0) % (hd // ch)
    gates_all = gt_ref[...]
    col_id = lax.broadcasted_iota(jnp.int32, gates_all.shape, 1)
    lane_id = lax.broadcasted_iota(jnp.int32, (nch, hd), 1)
    for d in range(2):
        rb_ = jnp.sum(jnp.where(col_id == cols[d], gates_all, 0.0), axis=1)
        rg_ = jnp.sum(jnp.where(col_id == cols[2 + d], gates_all, 0.0), axis=1)
        ours_b = jnp.where(half == 0, rb_, pltpu.roll(rb_, ch, 1))
        ours_g = jnp.where(half == 0, pltpu.roll(rg_, ch, 1), rg_)
        wide_ref[d] = jnp.where(lane_id < ch, ours_b, ours_g)

    def prep_load(c, d):
        r0 = pl.multiple_of(c * ch, ch)
        return (d, qs_ref[pl.ds(r0, ch), :], ks_ref[pl.ds(r0, ch), :], vs_ref[pl.ds(r0, ch), :],
                wide_ref[d, pl.ds(c, 1), :])

    def prep_compute(loaded, tick):
        chains = []
        for v in loaded:
            both = _bdot_nt(jnp.concatenate([v[2], v[1]], axis=0), v[2])
            chains.append(dict(d=v[0], kk=both[:ch], qk=both[ch:]))
        tick()
        for cn, v in zip(chains, loaded):
            d, qc, kc, vc, wide = v
            beta, gc = col_forms(wide)
            gr = wide[:, ch:]
            kf, qf, vc = kc.astype(F32), qc.astype(F32), vc.astype(F32)
            mask = masks[d]
            decay = jnp.where(mask, jnp.exp(jnp.where(mask, gc[:, :ch] - gr, 0.0)), 0.0)
            a = jnp.where(stricts[d], beta[:, :ch] * cn["kk"] * decay, 0.0)
            a_d = jnp.where(same_blk, a, 0.0)
            egc = jnp.exp(gc)
            g_last = gc[last_rows[d]:last_rows[d] + 1, :]
            cn.update(a_d=a_d, dinv=eye - a_d,
                      rhs=jnp.concatenate([kf * (beta * egc), vc * beta, a - a_d], axis=-1),
                      attn=jnp.where(mask, cn["qk"] * decay, 0.0), qd=qf * egc,
                      kt=kf * jnp.exp(g_last - gc), el=jnp.exp(g_last))
        for cn in chains:
            cn["p"] = _bdot(cn["a_d"], cn["a_d"])
        tick()
        for lvl in range(1, n_sub_lvl):
            for cn in chains:
                if lvl + 1 < n_sub_lvl:
                    both = _bdot(jnp.concatenate([cn["dinv"], cn["p"]], axis=0), cn["p"])
                    cn["dinv"] = cn["dinv"] + both[:ch]
                    cn["p"] = both[ch:]
                else:
                    cn["dinv"] = cn["dinv"] + _bdot(cn["dinv"], cn["p"])
            tick()
        for cn in chains:
            first = _bdot(cn["dinv"], cn["rhs"])
            cn["sol"] = first[:, :2 * hd]
            cn["pw"] = first[:, 2 * hd:]
        tick()
        for lvl in range(n_blk_lvl):
            for cn in chains:
                upd = _bdot(cn["pw"], cn["sol"])
                cn["sol"] = cn["sol"] - upd if lvl == 0 else cn["sol"] + upd
                if lvl + 1 < n_blk_lvl:
                    cn["pw"] = _bdot(cn["pw"], cn["pw"])
            tick()
        for cn in chains:
            cn["kta"] = jnp.concatenate([cn["kt"].T, cn["attn"]], axis=0)
        tick()
        for cn in chains:
            cn["x"] = _bdot(cn["kta"], cn["sol"])
        tick()
        outs = []
        for cn in chains:
            x = cn["x"]
            outs.append((jnp.concatenate([-x[:hd, :hd], cn["qd"] - x[hd:, :hd]], axis=0).astype(BF16),
                         x[:hd, hd:], x[hd:, hd:], cn["el"]))
        return outs

    def prep_store(c, d, out):
        r0 = pl.multiple_of(c * ch, ch)
        mq_ref[d, c] = out[0]
        n_ref[d, c] = out[1]
        o_ref[0, pl.ds(r0, ch), :] += out[2]
        el_ref[d, pl.ds(c, 1), :] = out[3]

    n_iter = nch // na_par

    def chunk_of(i, j, d):
        return i * na_par + j if d == 0 else nch - 1 - (i * na_par + j)

    def prep_set(i, tick):
        todo = [(chunk_of(i, j, d), d) for j in range(na_par) for d in range(2)]
        outs = prep_compute([prep_load(c, d) for c, d in todo], tick)
        for (c, d), out in zip(todo, outs):
            prep_store(c, d, out)


    def state_step(i, j, states):
        new = []
        for d in range(2):
            c = chunk_of(i, j, d)
            r0 = pl.multiple_of(c * ch, ch)
            r = jnp.dot(mq_ref[d, c], states[d].astype(BF16), preferred_element_type=F32)
            o_ref[0, pl.ds(r0, ch), :] += r[hd:]
            new.append(states[d] * el_ref[d, pl.ds(c, 1), :] + r[:hd] + n_ref[d, c])
        return new

    first_blocks = sorted(set(range(set_blocks)) | set(range(nrb - set_blocks, nrb)))
    for blk in first_blocks:
        for which in range(len(srcs)):
            conv_rows(which, blk * rb)
    pending = [(which, blk * rb) for blk in range(nrb) if blk not in first_blocks
               for which in range(len(srcs))]

    def conv_tick(per_tick=4):
        for _ in range(min(per_tick, len(pending))):
            conv_rows(*pending.pop(0))
    prep_set(0, conv_tick)
    while pending:
        conv_tick()

    def pipelined(i, carry):
        box = dict(states=list(carry), j=0)

        def tick():
            if box["j"] < na_par:
                box["states"] = state_step(i - 1, box["j"], box["states"])
                box["j"] += 1
        prep_set(i, tick)
        while box["j"] < na_par:
            tick()
        return tuple(box["states"])

    z = jnp.zeros((hd, hd), F32)
    states = lax.fori_loop(1, n_iter, pipelined, (z, z))
    states = list(states)
    for j in range(na_par):
        states = state_step(n_iter - 1, j, states)


def _delta_net(qkv, conv_w, gates_t, nh):
    bsz, seq, _ = qkv.shape
    hd = DN_HEAD_DIM
    ch = DN_CHUNK
    nch = seq // ch
    ng = gates_t.shape[1]
    per_blk = LANES // ch
    col = lambda off: pl.BlockSpec((1, seq, hd), lambda b, h: (b, 0, off + h))
    cw = lambda off: pl.BlockSpec((CONV_K, hd), lambda b, h: (0, off + h))
    return pl.pallas_call(
        functools.partial(_dn_body, seq=seq, ch=ch, nh=nh),
        grid=(bsz, nh),
        in_specs=[col(0), col(nh), col(2 * nh), cw(0), cw(nh), cw(2 * nh),
                  pl.BlockSpec((nch, ng, LANES), lambda b, h: (0, 0, b // per_blk))],
        out_specs=pl.BlockSpec((1, seq, hd), lambda b, h: (b, 0, h)),
        out_shape=jax.ShapeDtypeStruct((bsz, seq, nh * hd), F32),
        scratch_shapes=[pltpu.VMEM((seq + 16, hd), F32),
                        pltpu.VMEM((seq + 16, hd), F32),
                        pltpu.VMEM((seq + 16, hd), F32),
                        pltpu.VMEM((seq, hd), BF16),
                        pltpu.VMEM((seq, hd), BF16),
                        pltpu.VMEM((seq, hd), BF16),
                        pltpu.VMEM((2, nch, hd), F32),
                        pltpu.VMEM((2, nch, hd + ch, hd), BF16),
                        pltpu.VMEM((2, nch, hd, hd), F32),
                        pltpu.VMEM((2, nch, hd), F32)],
        compiler_params=_cparams(("arbitrary", "arbitrary")),
        name="delta_net",
    )(qkv, qkv, qkv, conv_w, conv_w, conv_w, gates_t)


def _out_body(x_ref, mod_ref, yt_ref, za_ref, o_ref, zb_ref, gw_ref, gbias_ref, dnw_ref,
              wo_ref, fw_ref, out_ref, slab_ref, *, nh, t_len):
    bsz, tt, d = x_ref.shape
    rows = bsz * tt
    d_ssm = za_ref.shape[-1]
    hd = o_ref.shape[-1] // nh
    n_tile = d_ssm // LANES
    g_per_tile = LANES // SSM_GROUP

    mix = jnp.zeros((rows, d), F32)
    for hh in range(nh):
        o = o_ref[:, :, hh * hd:(hh + 1) * hd].reshape(rows, hd)
        zb = zb_ref[:, :, hh * hd:(hh + 1) * hd].reshape(rows, hd).astype(F32)
        o = o * lax.rsqrt(jnp.mean(o * o, axis=-1, keepdims=True) + NORM_EPS) * dnw_ref[...]
        yb = (o * (zb * jax.nn.sigmoid(zb))).astype(BF16)
        mix = mix + jnp.dot(yb, wo_ref[pl.ds(d_ssm + hh * hd, hd), :], preferred_element_type=F32)

    grp = _lane_group((bsz, LANES))
    for cc in range(tt // t_len):
        for j in range(n_tile):
            for a in range(t_len // g_per_tile):
                ys = [yt_ref[j * g_per_tile + g8, cc, :, a * LANES:(a + 1) * LANES].astype(F32)
                      for g8 in range(g_per_tile)]
                for tp in range(g_per_tile):
                    tile = jnp.zeros((bsz, LANES), F32)
                    for g8 in range(g_per_tile):
                        sh = (SSM_GROUP * (g8 - tp)) % LANES
                        tile = jnp.where(grp == g8, pltpu.roll(ys[g8], sh, 1) if sh else ys[g8], tile)
                    t = a * g_per_tile + tp
                    slab_ref[j, pl.ds(cc * t_len + t, bsz, stride=tt), :] = tile
    ysn = jnp.concatenate([slab_ref[j] for j in range(n_tile)], axis=-1).astype(BF16)

    glu = jnp.dot(ysn, gw_ref[...], preferred_element_type=F32) + gbias_ref[...]
    za = za_ref[...].reshape(rows, d_ssm).astype(F32)
    ya = glu[:, :d_ssm] * jax.nn.sigmoid(glu[:, d_ssm:]) * (za * jax.nn.sigmoid(za))
    mix = mix + jnp.dot(ya.astype(BF16), wo_ref[pl.ds(0, d_ssm), :], preferred_element_type=F32)
    r = x_ref[...] + mod_ref[:, 2:3, :] * mix.reshape(bsz, tt, d)
    out_ref[...] = r * lax.rsqrt(jnp.mean(r * r, axis=-1, keepdims=True) + NORM_EPS) * fw_ref[...]


def _out_proj(x, mod3, yt, za, o, zb, glu_w, glu_b, dn_norm_w, w_out, final_norm_w, nh, t_len):
    bsz, seq, d = x.shape
    tt = 64
    g = yt.shape[0]
    full = lambda a: pl.BlockSpec(a.shape, lambda i: (0,) * a.ndim)
    row = lambda a: pl.BlockSpec((bsz, tt, a.shape[-1]), lambda i: (0, i, 0))
    gw = glu_w.astype(BF16)
    gbias = glu_b.reshape(1, -1)
    dnw = dn_norm_w.reshape(1, -1)
    wo = w_out.astype(BF16)
    fw = final_norm_w.reshape(1, -1)
    return pl.pallas_call(
        functools.partial(_out_body, nh=nh, t_len=t_len),
        grid=(seq // tt,),
        in_specs=[row(x), full(mod3),
                  pl.BlockSpec((g, tt // t_len, bsz, yt.shape[-1]), lambda i: (0, i, 0, 0)),
                  row(za), row(o), row(zb),
                  full(gw), full(gbias), full(dnw), full(wo), full(fw)],
        out_specs=row(x),
        out_shape=jax.ShapeDtypeStruct(x.shape, F32),
        scratch_shapes=[pltpu.VMEM((za.shape[-1] // LANES, bsz * tt, LANES), F32)],
        compiler_params=_cparams(("arbitrary",)),
        name="out_proj",
    )(x, mod3, yt, za, o, zb, gw, gbias, dnw, wo, fw)


def _layer(x, c, ada_w, ada_b, norm_w, w_in, conv_w,
           dn_a_log_f, dn_dt_bias_f, dn_a_log_b, dn_dt_bias_b, dn_norm_w,
           s5f, s5b, s5_d, glu_w, glu_b, w_out, out_norm_w):
    bsz, seq, d = x.shape
    d_ssm = s5_d.shape[0]
    d_dn = w_out.shape[0] - d_ssm
    nh = d_dn // DN_HEAD_DIM
    n_gate = 4 * nh
    p = SSM_GROUP
    g = d_ssm // p
    t = S5_T
    nc = seq // t

    mod3 = _ada_mod(c, ada_w, ada_b).reshape(bsz, 3, d)
    zeros = jnp.zeros((2 * nh,), F32)
    gparams = jnp.stack([
        jnp.concatenate([zeros, dn_a_log_f.astype(F32), dn_a_log_b.astype(F32)]),
        jnp.concatenate([zeros, dn_dt_bias_f.astype(F32), dn_dt_bias_b.astype(F32)])])
    gparams = jnp.pad(gparams.T, ((0, 0), (0, LANES - 2)))
    ut, za, qkv, zb, gates_t = _in_proj(x, mod3, norm_w, w_in, gparams, d_ssm, d_dn, n_gate, t)

    mi, bm, cm, lr, li = _s5_operators(s5f, s5b, s5_d, t)
    yt = _s5_mix(ut.reshape(g, nc * bsz, t * p), mi, bm, cm, lr, li, nc, bsz)
    yt = yt.reshape(g, nc, bsz, t * p)

    o = _delta_net(qkv, conv_w.astype(F32), gates_t, nh)

    return _out_proj(x, mod3, yt, za, o, zb, glu_w, glu_b, dn_norm_w, w_out, out_norm_w, nh, t)


def kernel(x, c, ada_w, ada_b, norm_w, w_in, conv_w, dn_a_log_f, dn_dt_bias_f, dn_a_log_b,
           dn_dt_bias_b, dn_norm_w, lam_re_f, lam_im_f, log_step_f, b_re_f, b_im_f, c_re_f, c_im_f,
           lam_re_b, lam_im_b, log_step_b, b_re_b, b_im_b, c_re_b, c_im_b, s5_d, glu_w, glu_b,
           w_out, final_norm_w):
    assert ada_w.shape[0] == 1, "single-layer block"
    s5f = (lam_re_f[0], lam_im_f[0], log_step_f[0], b_re_f[0], b_im_f[0], c_re_f[0], c_im_f[0])
    s5b = (lam_re_b[0], lam_im_b[0], log_step_b[0], b_re_b[0], b_im_b[0], c_re_b[0], c_im_b[0])
    return _layer(x, c, ada_w[0], ada_b[0], norm_w[0], w_in[0], conv_w[0],
                  dn_a_log_f[0], dn_dt_bias_f[0], dn_a_log_b[0], dn_dt_bias_b[0], dn_norm_w[0],
                  s5f, s5b, s5_d[0], glu_w[0], glu_b[0], w_out[0], final_norm_w)
```
